```python
import math, functools
import jax, jax.numpy as jnp
from jax import lax
import numpy as np

D_MODEL = 1024
BATCH = 2
SEQ = 8192
DEPTH = 4

BLOCK = 128
ROPE_THETA = 10000.0
RMS_EPS = 1e-6
NEG_INF = -1e30
GRID_W = 64

N_HEADS_A = 8
N_KV_A = 2
HEAD_DIM_A = 64
GROUP_A = N_HEADS_A // N_KV_A
WINDOW = 128
A_Q = N_HEADS_A * HEAD_DIM_A
A_KV = N_KV_A * HEAD_DIM_A

POOL_WINDOWS = (2, 4, 8, 16)
N_POOL_GROUPS = 4
POOL_WIDTH = D_MODEL // 2
POOL_GROUP = POOL_WIDTH // N_POOL_GROUPS

EVEN_IN = A_Q + 2 * A_KV + POOL_WIDTH
EVEN_OUT = A_Q + POOL_WIDTH

N_HEADS_C = 8
N_KV_C = 2
HEAD_DIM_C = 128
GROUP_C = N_HEADS_C // N_KV_C
C_Q = N_HEADS_C * HEAD_DIM_C
C_KV = N_KV_C * HEAD_DIM_C
ODD_IN = C_Q + 2 * C_KV
AXIAL_DIM = HEAD_DIM_C // 2

N_EXPERTS = 16
EXPERT_FF = 2048
CAPACITY_FACTOR = 2

N_EVEN = (DEPTH + 1) // 2
N_ODD = DEPTH // 2

kernel_name = "hybrid_swa_pool_axialattn_ecmoe_encoder"


def rms_norm(x, g):
    xf = x.astype(jnp.float32)
    y = xf * lax.rsqrt(jnp.mean(xf * xf, axis=-1, keepdims=True) + RMS_EPS)
    return (y * g.astype(jnp.float32)).astype(x.dtype)


def rope_cos_sin(pos, dim):
    inv = ROPE_THETA ** (-jnp.arange(0, dim, 2, dtype=jnp.float32) / dim)
    ang = pos.astype(jnp.float32)[:, None] * inv[None, :]
    return jnp.cos(ang), jnp.sin(ang)


def apply_rope(x, cos, sin):
    half = x.shape[-1] // 2
    x1, x2 = x[..., :half], x[..., half:]
    c = cos[None, :, None, :].astype(x.dtype)
    s = sin[None, :, None, :].astype(x.dtype)
    return jnp.concatenate([x1 * c - x2 * s, x1 * s + x2 * c], axis=-1)


def apply_axial_rope(x, cos_r, sin_r, cos_c, sin_c):
    xr, xc = x[..., :AXIAL_DIM], x[..., AXIAL_DIM:]
    return jnp.concatenate([apply_rope(xr, cos_r, sin_r), apply_rope(xc, cos_c, sin_c)], axis=-1)


def windowed_sink_attention(q, k, v, sink):
    b, s, _, _ = q.shape
    nb = s // BLOCK
    qb = q.reshape(b, nb, BLOCK, N_KV_A, GROUP_A, HEAD_DIM_A)
    pad = ((0, 0), (BLOCK, BLOCK), (0, 0), (0, 0))
    kp = jnp.pad(k, pad).reshape(b, nb + 2, BLOCK, N_KV_A, HEAD_DIM_A)
    vp = jnp.pad(v, pad).reshape(b, nb + 2, BLOCK, N_KV_A, HEAD_DIM_A)
    kw = jnp.concatenate([kp[:, :-2], kp[:, 1:-1], kp[:, 2:]], axis=2)
    vw = jnp.concatenate([vp[:, :-2], vp[:, 1:-1], vp[:, 2:]], axis=2)
    scores = jnp.einsum('bnqhgd,bnkhd->bnhgqk', qb, kw).astype(jnp.float32) * (HEAD_DIM_A ** -0.5)
    blk = jnp.arange(nb, dtype=jnp.int32)[:, None] * BLOCK
    qpos = blk + jnp.arange(BLOCK, dtype=jnp.int32)[None, :]
    kpos = blk - BLOCK + jnp.arange(3 * BLOCK, dtype=jnp.int32)[None, :]
    dist = qpos[:, :, None] - kpos[:, None, :]
    valid = (jnp.abs(dist) <= WINDOW) & (kpos[:, None, :] >= 0) & (kpos[:, None, :] < s)
    scores = jnp.where(valid[None, :, None, None, :, :], scores, NEG_INF)
    sink_col = jnp.broadcast_to(sink.astype(jnp.float32).reshape(1, 1, N_KV_A, GROUP_A, 1, 1),
                                scores.shape[:-1] + (1,))
    probs = jax.nn.softmax(jnp.concatenate([scores, sink_col], axis=-1), axis=-1)[..., :-1]
    out = jnp.einsum('bnhgqk,bnkhd->bnqhgd', probs.astype(v.dtype), vw)
    return out.reshape(b, s, A_Q)


def multiscale_pool(u, w_pool, scale):
    b, s, _ = u.shape
    uf = u.astype(jnp.float32).reshape(b, s, N_POOL_GROUPS, POOL_GROUP).transpose(0, 2, 1, 3)
    csum = jnp.pad(jnp.cumsum(uf, axis=2), ((0, 0), (0, 0), (1, 0), (0, 0)))
    half = jnp.array(POOL_WINDOWS, dtype=jnp.int32)[:, None] // 2
    t = jnp.arange(s, dtype=jnp.int32)[None, :]
    lo = jnp.clip(t - half, 0, s)
    hi = jnp.clip(t + half, 0, s)
    gi = jnp.arange(N_POOL_GROUPS, dtype=jnp.int32)[:, None]
    win_sum = csum[:, gi, hi, :] - csum[:, gi, lo, :]
    count = (hi - lo).astype(jnp.float32)[None, :, :, None]
    mixed = win_sum / count - uf
    y = jnp.einsum('bgsc,gcd->bsgd', mixed, w_pool.astype(jnp.float32)).reshape(b, s, POOL_WIDTH)
    return (y * scale.astype(jnp.float32)).astype(u.dtype)


def dense_block_attention(q, k, v):
    b, s, _, _ = q.shape
    nb = s // BLOCK
    qb = q.reshape(b, nb, BLOCK, N_KV_C, GROUP_C, HEAD_DIM_C).transpose(1, 0, 2, 3, 4, 5)

    def one_block(qblk):
        sc = jnp.einsum('bqhgd,bkhd->bhgqk', qblk, k).astype(jnp.float32) * (HEAD_DIM_C ** -0.5)
        p = jax.nn.softmax(sc, axis=-1)
        return jnp.einsum('bhgqk,bkhd->bqhgd', p.astype(v.dtype), v)

    out = lax.map(one_block, qb)
    return out.transpose(1, 0, 2, 3, 4, 5).reshape(b, s, C_Q)


def even_mixer(h, w_in, w_out, sink, w_pool, pool_scale, cos_a, sin_a):
    b, s, _ = h.shape
    proj = h @ w_in
    q, k, v, u = jnp.split(proj, [A_Q, A_Q + A_KV, A_Q + 2 * A_KV], axis=-1)
    q = apply_rope(q.reshape(b, s, N_HEADS_A, HEAD_DIM_A), cos_a, sin_a)
    k = apply_rope(k.reshape(b, s, N_KV_A, HEAD_DIM_A), cos_a, sin_a)
    v = v.reshape(b, s, N_KV_A, HEAD_DIM_A)
    attn = windowed_sink_attention(q, k, v, sink)
    pool = multiscale_pool(u, w_pool, pool_scale)
    return jnp.concatenate([attn, pool], axis=-1) @ w_out


def odd_mixer(h, w_qkv, q_gain, k_gain, w_out, cos_r, sin_r, cos_c, sin_c):
    b, s, _ = h.shape
    proj = h @ w_qkv
    q, k, v = jnp.split(proj, [C_Q, C_Q + C_KV], axis=-1)
    q = rms_norm(q.reshape(b, s, N_HEADS_C, HEAD_DIM_C), q_gain)
    k = rms_norm(k.reshape(b, s, N_KV_C, HEAD_DIM_C), k_gain)
    q = apply_axial_rope(q, cos_r, sin_r, cos_c, sin_c)
    k = apply_axial_rope(k, cos_r, sin_r, cos_c, sin_c)
    v = v.reshape(b, s, N_KV_C, HEAD_DIM_C)
    return dense_block_attention(q, k, v) @ w_out


def expert_choice_ffn(h, w_router, w_gate, w_up, w_down):
    b, s, d = h.shape
    cap = CAPACITY_FACTOR * s // N_EXPERTS
    aff = jax.nn.softmax(jnp.einsum('bsd,de->bse', h, w_router).astype(jnp.float32), axis=-1)
    gate, idx = lax.top_k(aff.transpose(0, 2, 1), cap)
    xs = jax.vmap(lambda hb, ib: hb[ib])(h, idx)
    g = jnp.einsum('becd,edf->becf', xs, w_gate)
    u = jnp.einsum('becd,edf->becf', xs, w_up)
    y = jnp.einsum('becf,efd->becd', jax.nn.silu(g) * u, w_down) * gate[..., None].astype(h.dtype)
    out = jax.vmap(lambda ib, yb: jnp.zeros((s, d), yb.dtype).at[ib.reshape(-1)].add(yb.reshape(-1, d)))(idx, y)
    return out


def setup_inputs(seed: int = 0) -> dict:
    key = jax.random.key(seed)
    ks = jax.random.split(key, 20)
    f32 = jnp.float32
    nrm = lambda k, shape, scale: jax.random.normal(k, shape, f32) * scale
    return {
        "x": nrm(ks[0], (BATCH, SEQ, D_MODEL), 1.0),
        "norm_mix": 1.0 + nrm(ks[1], (DEPTH, D_MODEL), 0.02),
        "norm_ffn": 1.0 + nrm(ks[2], (DEPTH, D_MODEL), 0.02),
        "norm_final": 1.0 + nrm(ks[3], (D_MODEL,), 0.02),
        "a_w_in": nrm(ks[4], (N_EVEN, D_MODEL, EVEN_IN), D_MODEL ** -0.5),
        "a_w_out": nrm(ks[5], (N_EVEN, EVEN_OUT, D_MODEL), EVEN_OUT ** -0.5),
        "a_sink": nrm(ks[6], (N_EVEN, N_HEADS_A), 0.5),
        "b_w_pool": nrm(ks[7], (N_EVEN, N_POOL_GROUPS, POOL_GROUP, POOL_GROUP), POOL_GROUP ** -0.5),
        "b_scale": 1.0 + nrm(ks[8], (N_EVEN, POOL_WIDTH), 0.02),
        "c_w_qkv": nrm(ks[9], (N_ODD, D_MODEL, ODD_IN), D_MODEL ** -0.5),
        "c_q_norm": 1.0 + nrm(ks[10], (N_ODD, HEAD_DIM_C), 0.02),
        "c_k_norm": 1.0 + nrm(ks[11], (N_ODD, HEAD_DIM_C), 0.02),
        "c_w_out": nrm(ks[12], (N_ODD, C_Q, D_MODEL), C_Q ** -0.5),
        "moe_router": nrm(ks[13], (DEPTH, D_MODEL, N_EXPERTS), D_MODEL ** -0.5),
        "moe_w_gate": nrm(ks[14], (DEPTH, N_EXPERTS, D_MODEL, EXPERT_FF), D_MODEL ** -0.5),
        "moe_w_up": nrm(ks[15], (DEPTH, N_EXPERTS, D_MODEL, EXPERT_FF), D_MODEL ** -0.5),
        "moe_w_down": nrm(ks[16], (DEPTH, N_EXPERTS, EXPERT_FF, D_MODEL), EXPERT_FF ** -0.5),
    }


def reference(x, norm_mix, norm_ffn, norm_final, a_w_in, a_w_out, a_sink, b_w_pool, b_scale,
              c_w_qkv, c_q_norm, c_k_norm, c_w_out, moe_router, moe_w_gate, moe_w_up, moe_w_down):
    s = x.shape[1]
    rows = s // GRID_W
    t = jnp.arange(s, dtype=jnp.int32)
    cos_a, sin_a = rope_cos_sin(t, HEAD_DIM_A)
    row_idx = jnp.repeat(jnp.arange(rows, dtype=jnp.int32), GRID_W)
    col_idx = jnp.tile(jnp.arange(GRID_W, dtype=jnp.int32), rows)
    cos_r, sin_r = rope_cos_sin(row_idx, AXIAL_DIM)
    cos_c, sin_c = rope_cos_sin(col_idx, AXIAL_DIM)
    for i in range(DEPTH):
        j = i // 2
        h = rms_norm(x, norm_mix[i])
        if i % 2 == 0:
            x = x + even_mixer(h, a_w_in[j], a_w_out[j], a_sink[j], b_w_pool[j], b_scale[j], cos_a, sin_a)
        else:
            x = x + odd_mixer(h, c_w_qkv[j], c_q_norm[j], c_k_norm[j], c_w_out[j], cos_r, sin_r, cos_c, sin_c)
        h = rms_norm(x, norm_ffn[i])
        x = x + expert_choice_ffn(h, moe_router[i], moe_w_gate[i], moe_w_up[i], moe_w_down[i])
    return rms_norm(x, norm_final)
```

```python
import functools

import jax
import jax.numpy as jnp
from jax import lax
from jax.experimental import pallas as pl
from jax.experimental.pallas import tpu as pltpu

f32 = jnp.float32
bf16 = jnp.bfloat16
i32 = jnp.int32
u32 = jnp.uint32

ROPE_THETA = 10000.0
RMS_EPS = 1e-6
NEG_INF = -1e30
GRID_W = 64
N_HEADS_A, N_KV_A, HEAD_DIM_A = 8, 2, 64
WINDOW = 128
A_Q = N_HEADS_A * HEAD_DIM_A
A_KV = N_KV_A * HEAD_DIM_A
POOL_WINDOWS = (2, 4, 8, 16)
POOL_GROUP = 128
POOL_WIDTH = POOL_GROUP * len(POOL_WINDOWS)
N_HEADS_C, N_KV_C, HEAD_DIM_C = 8, 2, 128
GROUP_C = N_HEADS_C // N_KV_C
C_Q = N_HEADS_C * HEAD_DIM_C
C_KV = N_KV_C * HEAD_DIM_C
AXIAL_DIM = HEAD_DIM_C // 2
N_EXPERTS = 16
CAPACITY_FACTOR = 2

LANES = 128
SUBLANES = 8
ROW_TILE = 512
WIN_TQ = 256
FLASH_TQ = 512
FLASH_TK = 512
FF_TILE = 256
SLOT_CHUNK = 256
VMEM_LIMIT_FFN = 60 * 1024 * 1024


def _rms(x, g):
    ms = jnp.mean(x * x, axis=-1, keepdims=True)
    return x * lax.rsqrt(ms + RMS_EPS) * g


def _rot_half32(x):
    lane = lax.broadcasted_iota(i32, x.shape, 1)
    first = (lane & 63) < 32
    return jnp.where(first, pltpu.roll(x, LANES - 32, 1), pltpu.roll(x, 32, 1))


def _rope(x, cos, sin_signed):
    return x * cos + _rot_half32(x) * sin_signed


def _rope_tables(pos1, pos2, dim):
    inv = ROPE_THETA ** (-jnp.arange(0, dim, 2, dtype=f32) / dim)

    def cs(pos):
        ang = pos.astype(f32)[:, None] * inv[None, :]
        return jnp.cos(ang), jnp.sin(ang)

    c1, s1 = cs(pos1)
    c2, s2 = cs(pos2)
    return (jnp.concatenate([c1, c1, c2, c2], axis=-1),
            jnp.concatenate([-s1, s1, -s2, s2], axis=-1))


def _even_in_kernel(x_ref, g_ref, w_ref, cos_ref, sin_ref,
                    q_ref, ka_ref, kb_ref, va_ref, vb_ref, u_ref):
    h = _rms(x_ref[...], g_ref[...]).astype(bf16)
    proj = jnp.dot(h, w_ref[...], preferred_element_type=f32)
    cos = cos_ref[...]
    sin = sin_ref[...]
    scale = HEAD_DIM_A ** -0.5
    for c in range(A_Q // LANES):
        qc = proj[:, c * LANES:(c + 1) * LANES]
        q_ref[:, c * LANES:(c + 1) * LANES] = (_rope(qc, cos, sin) * scale).astype(bf16)
    kc = _rope(proj[:, A_Q:A_Q + A_KV], cos, sin)
    ka_ref[...] = kc.astype(bf16)
    kb_ref[...] = pltpu.roll(kc, HEAD_DIM_A, 1).astype(bf16)
    vc = proj[:, A_Q + A_KV:A_Q + 2 * A_KV]
    va_ref[...] = vc.astype(bf16)
    vb_ref[...] = pltpu.roll(vc, HEAD_DIM_A, 1).astype(bf16)
    u_ref[...] = proj[:, A_Q + 2 * A_KV:]


def _even_in(x2, g, w_bf, cos, sin, seq):
    n, d = x2.shape
    tm = ROW_TILE
    per_seq = seq // tm
    row = lambda i: (i, 0)
    tab = lambda i: (i % per_seq, 0)
    full = lambda i: (0, 0)
    return pl.pallas_call(
        _even_in_kernel,
        grid=(n // tm,),
        in_specs=[pl.BlockSpec((tm, d), row), pl.BlockSpec((1, d), full),
                  pl.BlockSpec(w_bf.shape, full),
                  pl.BlockSpec((tm, LANES), tab), pl.BlockSpec((tm, LANES), tab)],
        out_specs=[pl.BlockSpec((tm, A_Q), row)] + [pl.BlockSpec((tm, A_KV), row)] * 4
        + [pl.BlockSpec((tm, POOL_WIDTH), row)],
        out_shape=[jax.ShapeDtypeStruct((n, A_Q), bf16)] + [jax.ShapeDtypeStruct((n, A_KV), bf16)] * 4
        + [jax.ShapeDtypeStruct((n, POOL_WIDTH), f32)],
        compiler_params=pltpu.CompilerParams(dimension_semantics=("arbitrary",)),
        name="even_in_proj",
    )(x2, g, w_bf, cos, sin)


def _win_attn_kernel(sink_ref, q_ref, kap, kam, kan, kbp, kbm, kbn, vap, vam, van, vbp, vbm, vbn,
                     o_ref, *, seq):
    i = pl.program_id(1)
    tq = q_ref.shape[0]
    nk = tq + 2 * WINDOW
    cat = lambda p, m, n: jnp.concatenate([p[...], m[...], n[...]], axis=0)
    ka, kb, va, vb = cat(kap, kam, kan), cat(kbp, kbm, kbn), cat(vap, vam, van), cat(vbp, vbm, vbn)
    lo_k = lax.broadcasted_iota(i32, (nk, LANES), 1) < HEAD_DIM_A
    lo_q = lax.broadcasted_iota(i32, (tq, LANES), 1) < HEAD_DIM_A
    zero = jnp.zeros((nk, LANES), bf16)
    qpos = i * tq + lax.broadcasted_iota(i32, (tq, nk), 0)
    kpos = i * tq - WINDOW + lax.broadcasted_iota(i32, (tq, nk), 1)
    valid = (jnp.abs(qpos - kpos) <= WINDOW) & (kpos >= 0) & (kpos < seq)
    for c in range(A_Q // LANES):
        kv = c // (A_Q // LANES // N_KV_A)
        k_same, k_swap = (ka, kb) if kv == 0 else (kb, ka)
        v_same, v_swap = (va, vb) if kv == 0 else (vb, va)
        qc = q_ref[:, c * LANES:(c + 1) * LANES]
        halves = []
        for half in range(2):
            kz = jnp.where(lo_k, k_same, zero) if half == 0 else jnp.where(lo_k, zero, k_swap)
            vz = v_same if half == 0 else v_swap
            s = lax.dot_general(qc, kz, (((1,), (1,)), ((), ())), preferred_element_type=f32)
            s = jnp.where(valid, s, NEG_INF)
            sk = sink_ref[2 * c + half]
            m = jnp.maximum(jnp.max(s, axis=1, keepdims=True), sk)
            p = jnp.exp(s - m)
            l = jnp.sum(p, axis=1, keepdims=True) + jnp.exp(sk - m)
            pv = jnp.dot(p.astype(bf16), vz, preferred_element_type=f32)
            halves.append(pv / l)
        o_ref[:, c * LANES:(c + 1) * LANES] = jnp.where(lo_q, halves[0], halves[1]).astype(bf16)


def _win_attn(sink, q3, ka3, kb3, va3, vb3):
    b, seq, _ = q3.shape
    tq = WIN_TQ
    r = tq // WINDOW
    nwb = seq // WINDOW
    main = lambda bi, i: (bi, i, 0)
    prev = lambda bi, i: (bi, jnp.maximum(i * r - 1, 0), 0)
    nxt = lambda bi, i: (bi, jnp.minimum((i + 1) * r, nwb - 1), 0)
    kv_specs = [pl.BlockSpec((None, WINDOW, A_KV), prev), pl.BlockSpec((None, tq, A_KV), main),
                pl.BlockSpec((None, WINDOW, A_KV), nxt)]
    return pl.pallas_call(
        functools.partial(_win_attn_kernel, seq=seq),
        grid=(b, seq // tq),
        in_specs=[pl.BlockSpec(memory_space=pltpu.SMEM), pl.BlockSpec((None, tq, A_Q), main)] + kv_specs * 4,
        out_specs=pl.BlockSpec((None, tq, A_Q), main),
        out_shape=jax.ShapeDtypeStruct((b, seq, A_Q), bf16),
        compiler_params=pltpu.CompilerParams(dimension_semantics=("arbitrary", "arbitrary")),
        name="window_attention",
    )(sink, q3, ka3, ka3, ka3, kb3, kb3, kb3, va3, va3, va3, vb3, vb3, vb3)


def _post_mixer(xn, g_ref, wr_ref, xo_ref, hpk_ref, aff_ref):
    xo_ref[...] = xn
    h2 = _rms(xn, g_ref[...])
    half = h2.shape[1] // 2
    hb = h2.astype(bf16).astype(f32)
    lo = lax.bitcast_convert_type(hb[:, :half], u32) >> 16
    hi = lax.bitcast_convert_type(hb[:, half:], u32) & jnp.uint32(0xFFFF0000)
    hpk_ref[...] = lo | hi
    logits = jnp.dot(h2, wr_ref[...], preferred_element_type=f32, precision=lax.Precision.HIGHEST)
    lane = lax.broadcasted_iota(i32, logits.shape, 1)
    logits = jnp.where(lane < N_EXPERTS, logits, NEG_INF)
    e = jnp.exp(logits - jnp.max(logits, axis=1, keepdims=True))
    aff = e / jnp.sum(e, axis=1, keepdims=True)
    aff_ref[...] = aff.T[:N_EXPERTS, :]


def _post_specs(n, d, seq, tm, b):
    per_seq = seq // tm
    row = lambda i: (i, 0)
    out_specs = [pl.BlockSpec((tm, d), row), pl.BlockSpec((tm, d // 2), row),
                 pl.BlockSpec((None, N_EXPERTS, tm), lambda i: (i // per_seq, 0, i % per_seq))]
    out_shape = [jax.ShapeDtypeStruct((n, d), f32), jax.ShapeDtypeStruct((n, d // 2), u32),
                 jax.ShapeDtypeStruct((b, N_EXPERTS, seq), f32)]
    return out_specs, out_shape


def _even_out_kernel(attn_ref, um_ref, up_ref, un_ref, x_ref, wo_ref, wp_ref, sc_ref, g_ref, wr_ref,
                     xo_ref, hpk_ref, aff_ref, ubuf, *, seq):
    tm = attn_ref.shape[0]
    halo = SUBLANES
    pos0 = (pl.program_id(0) % (seq // tm)) * tm
    um = um_ref[...]
    ubuf[0:halo, :] = jnp.where(pos0 > 0, up_ref[...], 0.0)
    ubuf[halo:halo + tm, :] = um
    ubuf[halo + tm:, :] = jnp.where(pos0 + tm < seq, un_ref[...], 0.0)
    t = pos0 + lax.broadcasted_iota(i32, (tm, 1), 0)
    proj = jnp.dot(attn_ref[...], wo_ref[0:A_Q, :], preferred_element_type=f32)
    for g, w in enumerate(POOL_WINDOWS):
        hw = w // 2
        cols = slice(g * POOL_GROUP, (g + 1) * POOL_GROUP)
        acc = ubuf[halo - hw:halo - hw + tm, cols]
        for o in range(-hw + 1, hw):
            acc = acc + ubuf[halo + o:halo + o + tm, cols]
        cnt = (jnp.minimum(t + hw, seq) - jnp.maximum(t - hw, 0)).astype(f32)
        mixed = acc / cnt - um[:, cols]
        y = jnp.dot(mixed.astype(bf16), wp_ref[g], preferred_element_type=f32) * sc_ref[:, cols]
        proj = proj + jnp.dot(y.astype(bf16), wo_ref[A_Q + g * POOL_GROUP:A_Q + (g + 1) * POOL_GROUP, :],
                              preferred_element_type=f32)
    _post_mixer(x_ref[...] + proj, g_ref, wr_ref, xo_ref, hpk_ref, aff_ref)


def _even_out(attn2, u2, x2, wo_bf, wp_bf, scale, g, wr, seq, b):
    n, d = x2.shape
    tm = ROW_TILE
    halo = SUBLANES
    row = lambda i: (i, 0)
    full2 = lambda i: (0, 0)
    nhb = n // halo
    out_specs, out_shape = _post_specs(n, d, seq, tm, b)
    return pl.pallas_call(
        functools.partial(_even_out_kernel, seq=seq),
        grid=(n // tm,),
        in_specs=[pl.BlockSpec((tm, A_Q), row), pl.BlockSpec((tm, POOL_WIDTH), row),
                  pl.BlockSpec((halo, POOL_WIDTH), lambda i: (jnp.maximum(i * (tm // halo) - 1, 0), 0)),
                  pl.BlockSpec((halo, POOL_WIDTH), lambda i: (jnp.minimum((i + 1) * (tm // halo), nhb - 1), 0)),
                  pl.BlockSpec((tm, d), row), pl.BlockSpec(wo_bf.shape, full2),
                  pl.BlockSpec(wp_bf.shape, lambda i: (0, 0, 0)), pl.BlockSpec((1, POOL_WIDTH), full2),
                  pl.BlockSpec((1, d), full2), pl.BlockSpec(wr.shape, full2)],
        out_specs=out_specs, out_shape=out_shape,
        scratch_shapes=[pltpu.VMEM((tm + 2 * halo, POOL_WIDTH), f32)],
        compiler_params=pltpu.CompilerParams(dimension_semantics=("arbitrary",)),
        name="even_out_proj",
    )(attn2, u2, u2, u2, x2, wo_bf, wp_bf, scale, g, wr)


def _odd_in_kernel(x_ref, g_ref, w_ref, qg_ref, kg_ref, cos_ref, sin_ref, q_ref, k_ref, v_ref):
    h = _rms(x_ref[...], g_ref[...]).astype(bf16)
    proj = jnp.dot(h, w_ref[...], preferred_element_type=f32)
    cos = cos_ref[...]
    sin = sin_ref[...]
    scale = HEAD_DIM_C ** -0.5
    for c in range(N_HEADS_C):
        qc = _rms(proj[:, c * LANES:(c + 1) * LANES], qg_ref[...])
        q_ref[:, c * LANES:(c + 1) * LANES] = (_rope(qc, cos, sin) * scale).astype(bf16)
    for c in range(N_KV_C):
        kc = _rms(proj[:, C_Q + c * LANES:C_Q + (c + 1) * LANES], kg_ref[...])
        k_ref[:, c * LANES:(c + 1) * LANES] = _rope(kc, cos, sin).astype(bf16)
    v_ref[...] = proj[:, C_Q + C_KV:].astype(bf16)


def _odd_in(x2, g, w_bf, qg, kg, cos, sin, seq):
    n, d = x2.shape
    tm = ROW_TILE
    per_seq = seq // tm
    row = lambda i: (i, 0)
    tab = lambda i: (i % per_seq, 0)
    full = lambda i: (0, 0)
    return pl.pallas_call(
        _odd_in_kernel,
        grid=(n // tm,),
        in_specs=[pl.BlockSpec((tm, d), row), pl.BlockSpec((1, d), full), pl.BlockSpec(w_bf.shape, full),
                  pl.BlockSpec((1, HEAD_DIM_C), full), pl.BlockSpec((1, HEAD_DIM_C), full),
                  pl.BlockSpec((tm, LANES), tab), pl.BlockSpec((tm, LANES), tab)],
        out_specs=[pl.BlockSpec((tm, C_Q), row), pl.BlockSpec((tm, C_KV), row), pl.BlockSpec((tm, C_KV), row)],
        out_shape=[jax.ShapeDtypeStruct((n, C_Q), bf16), jax.ShapeDtypeStruct((n, C_KV), bf16),
                   jax.ShapeDtypeStruct((n, C_KV), bf16)],
        compiler_params=pltpu.CompilerParams(dimension_semantics=("arbitrary",)),
        name="odd_in_proj",
    )(x2, g, w_bf, qg, kg, cos, sin)


def _flash_kernel(q_ref, k_ref, v_ref, o_ref, m_sc, l_sc, acc_sc):
    ki = pl.program_id(3)

    @pl.when(ki == 0)
    def _():
        m_sc[...] = jnp.full(m_sc.shape, NEG_INF, f32)
        l_sc[...] = jnp.zeros(l_sc.shape, f32)
        acc_sc[...] = jnp.zeros(acc_sc.shape, f32)

    k = k_ref[...]
    v = v_ref[...]
    for g in range(GROUP_C):
        qg = q_ref[:, g * LANES:(g + 1) * LANES]
        s = lax.dot_general(qg, k, (((1,), (1,)), ((), ())), preferred_element_type=f32)
        m_prev = m_sc[g]
        m_new = jnp.maximum(m_prev, jnp.max(s, axis=1, keepdims=True))
        alpha = jnp.exp(m_prev - m_new)
        p = jnp.exp(s - m_new)
        l_sc[g] = alpha * l_sc[g] + jnp.sum(p, axis=1, keepdims=True)
        acc_sc[g] = alpha * acc_sc[g] + jnp.dot(p.astype(bf16), v, preferred_element_type=f32)
        m_sc[g] = m_new

    @pl.when(ki == pl.num_programs(3) - 1)
    def _():
        for g in range(GROUP_C):
            o_ref[:, g * LANES:(g + 1) * LANES] = (acc_sc[g] / l_sc[g]).astype(bf16)


def _flash(q3, k3, v3):
    b, seq, _ = q3.shape
    tq, tk = FLASH_TQ, FLASH_TK
    gw = GROUP_C * HEAD_DIM_C
    return pl.pallas_call(
        _flash_kernel,
        grid=(b, N_KV_C, seq // tq, seq // tk),
        in_specs=[pl.BlockSpec((None, tq, gw), lambda bi, j, qi, ki: (bi, qi, j)),
                  pl.BlockSpec((None, tk, HEAD_DIM_C), lambda bi, j, qi, ki: (bi, ki, j)),
                  pl.BlockSpec((None, tk, HEAD_DIM_C), lambda bi, j, qi, ki: (bi, ki, j))],
        out_specs=pl.BlockSpec((None, tq, gw), lambda bi, j, qi, ki: (bi, qi, j)),
        out_shape=jax.ShapeDtypeStruct((b, seq, C_Q), bf16),
        scratch_shapes=[pltpu.VMEM((GROUP_C, tq, 1), f32), pltpu.VMEM((GROUP_C, tq, 1), f32),
                        pltpu.VMEM((GROUP_C, tq, HEAD_DIM_C), f32)],
        compiler_params=pltpu.CompilerParams(
            dimension_semantics=("arbitrary", "arbitrary", "arbitrary", "arbitrary")),
        name="dense_attention",
    )(q3, k3, v3)


def _odd_out_kernel(attn_ref, x_ref, wo_ref, g_ref, wr_ref, xo_ref, hpk_ref, aff_ref):
    proj = jnp.dot(attn_ref[...], wo_ref[...], preferred_element_type=f32)
    _post_mixer(x_ref[...] + proj, g_ref, wr_ref, xo_ref, hpk_ref, aff_ref)


def _odd_out(attn2, x2, wo_bf, g, wr, seq, b):
    n, d = x2.shape
    tm = ROW_TILE
    row = lambda i: (i, 0)
    full2 = lambda i: (0, 0)
    out_specs, out_shape = _post_specs(n, d, seq, tm, b)
    return pl.pallas_call(
        _odd_out_kernel,
        grid=(n // tm,),
        in_specs=[pl.BlockSpec((tm, C_Q), row), pl.BlockSpec((tm, d), row), pl.BlockSpec(wo_bf.shape, full2),
                  pl.BlockSpec((1, d), full2), pl.BlockSpec(wr.shape, full2)],
        out_specs=out_specs, out_shape=out_shape,
        compiler_params=pltpu.CompilerParams(dimension_semantics=("arbitrary",)),
        name="odd_out_proj",
    )(attn2, x2, wo_bf, g, wr)


def _topk_kernel(aff_ref, idx_ref, cum_sc, *, cap):
    ne, r, _ = aff_ref.shape
    er = ne * r
    shift = r.bit_length() - 1
    keys = lax.bitcast_convert_type(aff_ref[...], i32)

    def count(mask):
        return jnp.sum(jnp.sum(mask.astype(f32), axis=2, keepdims=True), axis=1, keepdims=True)

    def bit_step(it, thr):
        cand = thr | jnp.left_shift(jnp.int32(1), 30 - it)
        return jnp.where(count(keys >= cand) >= cap, cand, thr)

    thr = lax.fori_loop(0, 31, bit_step, jnp.zeros((ne, 1, 1), i32))
    gt = keys > thr
    eq = keys == thr
    need = cap - count(gt)

    li = lax.broadcasted_iota(i32, (LANES, LANES), 0)
    lj = lax.broadcasted_iota(i32, (LANES, LANES), 1)
    ri = lax.broadcasted_iota(i32, (er, er), 0)
    rj = lax.broadcasted_iota(i32, (er, er), 1)
    rows_before = (((ri >> shift) == (rj >> shift)) & (rj < ri)).astype(bf16)

    def token_cumsum(mask3, inclusive):
        x = mask3.astype(bf16).reshape(er, LANES)
        tri = ((li <= lj) if inclusive else (li < lj)).astype(bf16)
        within = jnp.dot(x, tri, preferred_element_type=f32)
        rowtot = jnp.sum(x.astype(f32), axis=1, keepdims=True)
        before = jnp.dot(rows_before, jnp.broadcast_to(rowtot, (er, LANES)).astype(bf16),
                         preferred_element_type=f32)
        return (within + before).reshape(ne, r, LANES)

    sel = gt | (eq & (token_cumsum(eq, False) < need))
    cum_sc[...] = token_cumsum(sel, True)

    ones = jnp.ones((SUBLANES, LANES), bf16)
    cs = min(SLOT_CHUNK, cap)

    def per_expert(e, carry):
        for ch in range(cap // cs):
            slot = (ch * cs + lax.broadcasted_iota(i32, (cs, LANES), 0)).astype(f32)

            def per_row(rr, acc):
                return acc + jnp.where(cum_sc[e, pl.ds(rr, 1), :] <= slot, 1.0, 0.0)

            acc = lax.fori_loop(0, r, per_row, jnp.zeros((cs, LANES), f32))
            cnt = lax.dot_general(ones, acc.astype(bf16), (((1,), (1,)), ((), ())),
                                  preferred_element_type=f32)
            idx_ref[e, :, pl.ds(ch * cs, cs)] = cnt[0:1, :].astype(i32)
        return carry

    lax.fori_loop(0, ne, per_expert, 0)


def _topk(aff4, cap):
    b, ne, r, _ = aff4.shape
    assert r & (r - 1) == 0 and r <= 256 and r % SUBLANES == 0
    return pl.pallas_call(
        functools.partial(_topk_kernel, cap=cap),
        grid=(b,),
        in_specs=[pl.BlockSpec((None, ne, r, LANES), lambda bi: (bi, 0, 0, 0))],
        out_specs=pl.BlockSpec((None, ne, 1, cap), lambda bi: (bi, 0, 0, 0)),
        out_shape=jax.ShapeDtypeStruct((b, ne, 1, cap), i32),
        scratch_shapes=[pltpu.VMEM((ne, r, LANES), f32)],
        compiler_params=pltpu.CompilerParams(dimension_semantics=("arbitrary",)),
        name="expert_choice_topk",
    )(aff4)


def _ffn_kernel(idx_ref, idxn_ref, aff_ref, hpk_ref, x_ref, wg_ref, wu_ref, wd_ref, out_ref,
                acc, xsbuf, xs, yacc, gsem, xsem, *, seq, cap):
    b, e, j = pl.program_id(0), pl.program_id(1), pl.program_id(2)
    nb, ne, nj = pl.num_programs(0), pl.num_programs(1), pl.num_programs(2)
    n = b * ne + e
    slot = n % 2
    rows_per_step = cap // nj
    half = xs.shape[1] // 2

    def gather_rows(iref, batch, dst_slot, lo, cnt):
        def one(s, c):
            t = iref[0, s]
            pltpu.make_async_copy(hpk_ref.at[pl.ds(batch * seq + t, 1), :],
                                  xsbuf.at[dst_slot, pl.ds(s, 1), :], gsem.at[dst_slot]).start()
            return c
        lax.fori_loop(lo, lo + cnt, one, 0)

    def x_copy():
        return pltpu.make_async_copy(x_ref.at[pl.ds(b * seq, seq), :], acc, xsem.at[0])

    @pl.when((e == 0) & (j == 0))
    def _():
        x_copy().start()

    @pl.when((n == 0) & (j == 0))
    def _():
        gather_rows(idx_ref, b, slot, 0, cap)

    @pl.when(j == 0)
    def _():
        pltpu.make_async_copy(hpk_ref.at[pl.ds(0, cap), :], xsbuf.at[slot], gsem.at[slot]).wait()
        w = xsbuf[slot]
        xs[:, :half] = lax.bitcast_convert_type(w << 16, f32).astype(bf16)
        xs[:, half:] = lax.bitcast_convert_type(w & jnp.uint32(0xFFFF0000), f32).astype(bf16)
        yacc[...] = jnp.zeros(yacc.shape, f32)

    @pl.when(n + 1 < nb * ne)
    def _():
        gather_rows(idxn_ref, (n + 1) // ne, 1 - slot, j * rows_per_step, rows_per_step)

    xv = xs[...]
    gate = jnp.dot(xv, wg_ref[...].astype(bf16), preferred_element_type=f32)
    up = jnp.dot(xv, wu_ref[...].astype(bf16), preferred_element_type=f32)
    hm = (gate * jax.nn.sigmoid(gate) * up).astype(bf16)
    yacc[...] += jnp.dot(hm, wd_ref[...].astype(bf16), preferred_element_type=f32)

    @pl.when(j == nj - 1)
    def _():
        @pl.when(e == 0)
        def _():
            x_copy().wait()

        def one(s, c):
            t = idx_ref[0, s]
            acc[pl.ds(t, 1), :] += aff_ref[0, t] * yacc[pl.ds(s, 1), :]
            return c
        lax.fori_loop(0, cap, one, 0)

        @pl.when(e == ne - 1)
        def _():
            cp = pltpu.make_async_copy(acc, out_ref.at[pl.ds(b * seq, seq), :], xsem.at[1])
            cp.start()
            cp.wait()


def _ffn(idx3, aff3, hpk, x2, wg, wu, wd, layer, b, seq, cap):
    n, d = x2.shape
    ne = N_EXPERTS
    ff = wg.shape[-1]
    tf = FF_TILE
    last = b * ne - 1
    cur = lambda bi, e, j: (bi * ne + e, 0, 0)
    nxt = lambda bi, e, j: (jnp.minimum(bi * ne + e + 1, last), 0, 0)
    smem = pltpu.SMEM
    return pl.pallas_call(
        functools.partial(_ffn_kernel, seq=seq, cap=cap),
        grid=(b, ne, ff // tf),
        in_specs=[pl.BlockSpec((None, 1, cap), cur, memory_space=smem),
                  pl.BlockSpec((None, 1, cap), nxt, memory_space=smem),
                  pl.BlockSpec((None, 1, seq), cur, memory_space=smem),
                  pl.BlockSpec(memory_space=pl.ANY), pl.BlockSpec(memory_space=pl.ANY),
                  pl.BlockSpec((None, None, d, tf), lambda bi, e, j: (layer, e, 0, j)),
                  pl.BlockSpec((None, None, d, tf), lambda bi, e, j: (layer, e, 0, j)),
                  pl.BlockSpec((None, None, tf, d), lambda bi, e, j: (layer, e, j, 0))],
        out_specs=pl.BlockSpec(memory_space=pl.ANY),
        out_shape=jax.ShapeDtypeStruct((n, d), f32),
        scratch_shapes=[pltpu.VMEM((seq, d), f32), pltpu.VMEM((2, cap, d // 2), u32),
                        pltpu.VMEM((cap, d), bf16), pltpu.VMEM((cap, d), f32),
                        pltpu.SemaphoreType.DMA((2,)), pltpu.SemaphoreType.DMA((2,))],
        compiler_params=pltpu.CompilerParams(
            dimension_semantics=("arbitrary", "arbitrary", "arbitrary"),
            vmem_limit_bytes=VMEM_LIMIT_FFN),
        name="expert_ffn",
    )(idx3, idx3, aff3, hpk, x2, wg, wu, wd)


def _final_norm_kernel(x_ref, g_ref, o_ref):
    o_ref[...] = _rms(x_ref[...], g_ref[...])


def _final_norm(x2, g):
    n, d = x2.shape
    tm = ROW_TILE
    return pl.pallas_call(
        _final_norm_kernel,
        grid=(n // tm,),
        in_specs=[pl.BlockSpec((tm, d), lambda i: (i, 0)), pl.BlockSpec((1, d), lambda i: (0, 0))],
        out_specs=pl.BlockSpec((tm, d), lambda i: (i, 0)),
        out_shape=jax.ShapeDtypeStruct((n, d), f32),
        compiler_params=pltpu.CompilerParams(dimension_semantics=("arbitrary",)),
        name="final_norm",
    )(x2, g)


def kernel(x, norm_mix, norm_ffn, norm_final, a_w_in, a_w_out, a_sink, b_w_pool, b_scale,
           c_w_qkv, c_q_norm, c_k_norm, c_w_out, moe_router, moe_w_gate, moe_w_up, moe_w_down):
    b, seq, d = x.shape
    depth = norm_mix.shape[0]
    n = b * seq
    cap = CAPACITY_FACTOR * seq // N_EXPERTS
    assert seq % ROW_TILE == 0 and seq % FLASH_TQ == 0 and seq % FLASH_TK == 0 and seq % WIN_TQ == 0
    assert cap % (moe_w_gate.shape[-1] // FF_TILE) == 0 and cap % min(SLOT_CHUNK, cap) == 0

    t = jnp.arange(seq, dtype=i32)
    cos_a, sin_a = _rope_tables(t, t, HEAD_DIM_A)
    cos_c, sin_c = _rope_tables(t // GRID_W, t % GRID_W, AXIAL_DIM)
    wr_pad = jnp.pad(moe_router, ((0, 0), (0, 0), (0, LANES - N_EXPERTS)))

    x2 = x.reshape(n, d)
    for i in range(depth):
        j = i // 2
        g_mix = norm_mix[i].reshape(1, d)
        g_ffn = norm_ffn[i].reshape(1, d)
        if i % 2 == 0:
            q, ka, kb, va, vb, u = _even_in(x2, g_mix, a_w_in[j].astype(bf16), cos_a, sin_a, seq)
            r3 = lambda a: a.reshape(b, seq, a.shape[-1])
            attn = _win_attn(a_sink[j], r3(q), r3(ka), r3(kb), r3(va), r3(vb))
            x2, hpk, aff = _even_out(attn.reshape(n, A_Q), u, x2, a_w_out[j].astype(bf16),
                                     b_w_pool[j].astype(bf16), b_scale[j].reshape(1, POOL_WIDTH),
                                     g_ffn, wr_pad[i], seq, b)
        else:
            q, k, v = _odd_in(x2, g_mix, c_w_qkv[j].astype(bf16), c_q_norm[j].reshape(1, HEAD_DIM_C),
                              c_k_norm[j].reshape(1, HEAD_DIM_C), cos_c, sin_c, seq)
            attn = _flash(q.reshape(b, seq, C_Q), k.reshape(b, seq, C_KV), v.reshape(b, seq, C_KV))
            x2, hpk, aff = _odd_out(attn.reshape(n, C_Q), x2, c_w_out[j].astype(bf16), g_ffn, wr_pad[i], seq, b)
        idx = _topk(aff.reshape(b, N_EXPERTS, seq // LANES, LANES), cap)
        x2 = _ffn(idx.reshape(b * N_EXPERTS, 1, cap), aff.reshape(b * N_EXPERTS, 1, seq), hpk, x2,
                  moe_w_gate, moe_w_up, moe_w_down, i, b, seq, cap)
    return _final_norm(x2, norm_final.reshape(1, d)).reshape(b, seq, d)
```

```python
import functools

import jax
import jax.numpy as jnp
from jax import lax
from jax.experimental import pallas as pl
from jax.experimental.pallas import tpu as pltpu

f32 = jnp.float32
bf16 = jnp.bfloat16
i32 = jnp.int32
u32 = jnp.uint32

ROPE_THETA = 10000.0
RMS_EPS = 1e-6
NEG_INF = -1e30
GRID_W = 64
N_HEADS_A, N_KV_A, HEAD_DIM_A = 8, 2, 64
WINDOW = 128
A_Q = N_HEADS_A * HEAD_DIM_A
A_KV = N_KV_A * HEAD_DIM_A
POOL_WINDOWS = (2, 4, 8, 16)
POOL_GROUP = 128
POOL_WIDTH = POOL_GROUP * len(POOL_WINDOWS)
N_HEADS_C, N_KV_C, HEAD_DIM_C = 8, 2, 128
GROUP_C = N_HEADS_C // N_KV_C
C_Q = N_HEADS_C * HEAD_DIM_C
C_KV = N_KV_C * HEAD_DIM_C
AXIAL_DIM = HEAD_DIM_C // 2
N_EXPERTS = 16
CAPACITY_FACTOR = 2

LANES = 128
SUBLANES = 8
ROW_TILE = 512
WIN_TQ = 256
FLASH_TQ = 512
FLASH_TK = 2048
LOG2_E = 1.4426950408889634
FF_TILE = 256
ROW_UNROLL = 8
SCATTER_GROUP = 4
VMEM_LIMIT_FFN = 60 * 1024 * 1024


def _rms(x, g):
    ms = jnp.mean(x * x, axis=-1, keepdims=True)
    return x * lax.rsqrt(ms + RMS_EPS) * g


def _rot_half32(x):
    lane = lax.broadcasted_iota(i32, x.shape, 1)
    first = (lane & 63) < 32
    return jnp.where(first, pltpu.roll(x, LANES - 32, 1), pltpu.roll(x, 32, 1))


def _rope(x, cos, sin_signed):
    return x * cos + _rot_half32(x) * sin_signed


def _rope_tables(pos1, pos2, dim):
    inv = ROPE_THETA ** (-jnp.arange(0, dim, 2, dtype=f32) / dim)

    def cs(pos):
        ang = pos.astype(f32)[:, None] * inv[None, :]
        return jnp.cos(ang), jnp.sin(ang)

    c1, s1 = cs(pos1)
    c2, s2 = cs(pos2)
    return (jnp.concatenate([c1, c1, c2, c2], axis=-1),
            jnp.concatenate([-s1, s1, -s2, s2], axis=-1))


def _even_in_kernel(x_ref, g_ref, w_ref, cos_ref, sin_ref,
                    q_ref, ka_ref, kb_ref, va_ref, vb_ref, u_ref):
    h = _rms(x_ref[...], g_ref[...]).astype(bf16)
    proj = jnp.dot(h, w_ref[...], preferred_element_type=f32)
    cos = cos_ref[...]
    sin = sin_ref[...]
    scale = HEAD_DIM_A ** -0.5
    for c in range(A_Q // LANES):
        qc = proj[:, c * LANES:(c + 1) * LANES]
        q_ref[:, c * LANES:(c + 1) * LANES] = (_rope(qc, cos, sin) * scale).astype(bf16)
    kc = _rope(proj[:, A_Q:A_Q + A_KV], cos, sin)
    ka_ref[...] = kc.astype(bf16)
    kb_ref[...] = pltpu.roll(kc, HEAD_DIM_A, 1).astype(bf16)
    vc = proj[:, A_Q + A_KV:A_Q + 2 * A_KV]
    va_ref[...] = vc.astype(bf16)
    vb_ref[...] = pltpu.roll(vc, HEAD_DIM_A, 1).astype(bf16)
    u_ref[...] = proj[:, A_Q + 2 * A_KV:]


def _even_in(x2, g, w_bf, cos, sin, seq):
    n, d = x2.shape
    tm = ROW_TILE
    per_seq = seq // tm
    row = lambda i: (i, 0)
    tab = lambda i: (i % per_seq, 0)
    full = lambda i: (0, 0)
    return pl.pallas_call(
        _even_in_kernel,
        grid=(n // tm,),
        in_specs=[pl.BlockSpec((tm, d), row), pl.BlockSpec((1, d), full),
                  pl.BlockSpec(w_bf.shape, full),
                  pl.BlockSpec((tm, LANES), tab), pl.BlockSpec((tm, LANES), tab)],
        out_specs=[pl.BlockSpec((tm, A_Q), row)] + [pl.BlockSpec((tm, A_KV), row)] * 4
        + [pl.BlockSpec((tm, POOL_WIDTH), row)],
        out_shape=[jax.ShapeDtypeStruct((n, A_Q), bf16)] + [jax.ShapeDtypeStruct((n, A_KV), bf16)] * 4
        + [jax.ShapeDtypeStruct((n, POOL_WIDTH), f32)],
        compiler_params=pltpu.CompilerParams(dimension_semantics=("arbitrary",)),
        name="even_in_proj",
    )(x2, g, w_bf, cos, sin)


def _win_attn_kernel(sink_ref, q_ref, kap, kam, kan, kbp, kbm, kbn, vap, vam, van, vbp, vbm, vbn,
                     o_ref, *, seq):
    i = pl.program_id(1)
    tq = q_ref.shape[0]
    nk = tq + 2 * WINDOW
    cat = lambda p, m, n: jnp.concatenate([p[...], m[...], n[...]], axis=0)
    ka, kb, va, vb = cat(kap, kam, kan), cat(kbp, kbm, kbn), cat(vap, vam, van), cat(vbp, vbm, vbn)
    lo_k = lax.broadcasted_iota(i32, (nk, LANES), 1) < HEAD_DIM_A
    lo_q = lax.broadcasted_iota(i32, (tq, LANES), 1) < HEAD_DIM_A
    zero = jnp.zeros((nk, LANES), bf16)
    qpos = i * tq + lax.broadcasted_iota(i32, (tq, nk), 0)
    kpos = i * tq - WINDOW + lax.broadcasted_iota(i32, (tq, nk), 1)
    valid = (jnp.abs(qpos - kpos) <= WINDOW) & (kpos >= 0) & (kpos < seq)
    for c in range(A_Q // LANES):
        kv = c // (A_Q // LANES // N_KV_A)
        k_same, k_swap = (ka, kb) if kv == 0 else (kb, ka)
        v_same, v_swap = (va, vb) if kv == 0 else (vb, va)
        qc = q_ref[:, c * LANES:(c + 1) * LANES]
        halves = []
        for half in range(2):
            kz = jnp.where(lo_k, k_same, zero) if half == 0 else jnp.where(lo_k, zero, k_swap)
            vz = v_same if half == 0 else v_swap
            s = lax.dot_general(qc, kz, (((1,), (1,)), ((), ())), preferred_element_type=f32)
            s = jnp.where(valid, s, NEG_INF)
            sk = sink_ref[2 * c + half]
            m = jnp.maximum(jnp.max(s, axis=1, keepdims=True), sk)
            p = jnp.exp(s - m)
            l = jnp.sum(p, axis=1, keepdims=True) + jnp.exp(sk - m)
            pv = jnp.dot(p.astype(bf16), vz, preferred_element_type=f32)
            halves.append(pv / l)
        o_ref[:, c * LANES:(c + 1) * LANES] = jnp.where(lo_q, halves[0], halves[1]).astype(bf16)


def _win_attn(sink, q3, ka3, kb3, va3, vb3):
    b, seq, _ = q3.shape
    tq = WIN_TQ
    r = tq // WINDOW
    nwb = seq // WINDOW
    main = lambda bi, i: (bi, i, 0)
    prev = lambda bi, i: (bi, jnp.maximum(i * r - 1, 0), 0)
    nxt = lambda bi, i: (bi, jnp.minimum((i + 1) * r, nwb - 1), 0)
    kv_specs = [pl.BlockSpec((None, WINDOW, A_KV), prev), pl.BlockSpec((None, tq, A_KV), main),
                pl.BlockSpec((None, WINDOW, A_KV), nxt)]
    return pl.pallas_call(
        functools.partial(_win_attn_kernel, seq=seq),
        grid=(b, seq // tq),
        in_specs=[pl.BlockSpec(memory_space=pltpu.SMEM), pl.BlockSpec((None, tq, A_Q), main)] + kv_specs * 4,
        out_specs=pl.BlockSpec((None, tq, A_Q), main),
        out_shape=jax.ShapeDtypeStruct((b, seq, A_Q), bf16),
        compiler_params=pltpu.CompilerParams(dimension_semantics=("arbitrary", "arbitrary")),
        name="window_attention",
    )(sink, q3, ka3, ka3, ka3, kb3, kb3, kb3, va3, va3, va3, vb3, vb3, vb3)


def _post_mixer(xn, g_ref, wr_ref, xo_ref, hpk_ref, aff_ref):
    xo_ref[...] = xn
    h2 = _rms(xn, g_ref[...])
    hpk_ref[...] = h2
    logits = jnp.dot(h2, wr_ref[...], preferred_element_type=f32, precision=lax.Precision.HIGHEST)
    lane = lax.broadcasted_iota(i32, logits.shape, 1)
    logits = jnp.where(lane < N_EXPERTS, logits, NEG_INF)
    e = jnp.exp(logits - jnp.max(logits, axis=1, keepdims=True))
    aff = e / jnp.sum(e, axis=1, keepdims=True)
    aff_ref[...] = aff.T[:N_EXPERTS, :]


def _post_specs(n, d, seq, tm, b):
    per_seq = seq // tm
    row = lambda i: (i, 0)
    out_specs = [pl.BlockSpec((tm, d), row), pl.BlockSpec((tm, d), row),
                 pl.BlockSpec((None, N_EXPERTS, tm), lambda i: (i // per_seq, 0, i % per_seq))]
    out_shape = [jax.ShapeDtypeStruct((n, d), f32), jax.ShapeDtypeStruct((n, d), f32),
                 jax.ShapeDtypeStruct((b, N_EXPERTS, seq), f32)]
    return out_specs, out_shape


def _even_out_kernel(attn_ref, um_ref, up_ref, un_ref, x_ref, wo_ref, wp_ref, sc_ref, g_ref, wr_ref,
                     xo_ref, hpk_ref, aff_ref, ubuf, *, seq):
    tm = attn_ref.shape[0]
    halo = SUBLANES
    pos0 = (pl.program_id(0) % (seq // tm)) * tm
    um = um_ref[...]
    ubuf[0:halo, :] = jnp.where(pos0 > 0, up_ref[...], 0.0)
    ubuf[halo:halo + tm, :] = um
    ubuf[halo + tm:, :] = jnp.where(pos0 + tm < seq, un_ref[...], 0.0)
    t = pos0 + lax.broadcasted_iota(i32, (tm, 1), 0)
    proj = jnp.dot(attn_ref[...], wo_ref[0:A_Q, :], preferred_element_type=f32)
    for g, w in enumerate(POOL_WINDOWS):
        hw = w // 2
        cols = slice(g * POOL_GROUP, (g + 1) * POOL_GROUP)
        acc = ubuf[halo - hw:halo - hw + tm, cols]
        for o in range(-hw + 1, hw):
            acc = acc + ubuf[halo + o:halo + o + tm, cols]
        cnt = (jnp.minimum(t + hw, seq) - jnp.maximum(t - hw, 0)).astype(f32)
        mixed = acc / cnt - um[:, cols]
        y = jnp.dot(mixed.astype(bf16), wp_ref[g], preferred_element_type=f32) * sc_ref[:, cols]
        proj = proj + jnp.dot(y.astype(bf16), wo_ref[A_Q + g * POOL_GROUP:A_Q + (g + 1) * POOL_GROUP, :],
                              preferred_element_type=f32)
    _post_mixer(x_ref[...] + proj, g_ref, wr_ref, xo_ref, hpk_ref, aff_ref)


def _even_out(attn2, u2, x2, wo_bf, wp_bf, scale, g, wr, seq, b):
    n, d = x2.shape
    tm = ROW_TILE
    halo = SUBLANES
    row = lambda i: (i, 0)
    full2 = lambda i: (0, 0)
    nhb = n // halo
    out_specs, out_shape = _post_specs(n, d, seq, tm, b)
    return pl.pallas_call(
        functools.partial(_even_out_kernel, seq=seq),
        grid=(n // tm,),
        in_specs=[pl.BlockSpec((tm, A_Q), row), pl.BlockSpec((tm, POOL_WIDTH), row),
                  pl.BlockSpec((halo, POOL_WIDTH), lambda i: (jnp.maximum(i * (tm // halo) - 1, 0), 0)),
                  pl.BlockSpec((halo, POOL_WIDTH), lambda i: (jnp.minimum((i + 1) * (tm // halo), nhb - 1), 0)),
                  pl.BlockSpec((tm, d), row), pl.BlockSpec(wo_bf.shape, full2),
                  pl.BlockSpec(wp_bf.shape, lambda i: (0, 0, 0)), pl.BlockSpec((1, POOL_WIDTH), full2),
                  pl.BlockSpec((1, d), full2), pl.BlockSpec(wr.shape, full2)],
        out_specs=out_specs, out_shape=out_shape,
        scratch_shapes=[pltpu.VMEM((tm + 2 * halo, POOL_WIDTH), f32)],
        compiler_params=pltpu.CompilerParams(dimension_semantics=("arbitrary",)),
        name="even_out_proj",
    )(attn2, u2, u2, u2, x2, wo_bf, wp_bf, scale, g, wr)


def _odd_in_kernel(x_ref, g_ref, w_ref, qg_ref, kg_ref, cos_ref, sin_ref, q_ref, k_ref, v_ref):
    h = _rms(x_ref[...], g_ref[...]).astype(bf16)
    proj = jnp.dot(h, w_ref[...], preferred_element_type=f32)
    cos = cos_ref[...]
    sin = sin_ref[...]
    scale = HEAD_DIM_C ** -0.5 * LOG2_E
    for c in range(N_HEADS_C):
        qc = _rms(proj[:, c * LANES:(c + 1) * LANES], qg_ref[...])
        q_ref[:, c * LANES:(c + 1) * LANES] = (_rope(qc, cos, sin) * scale).astype(bf16)
    for c in range(N_KV_C):
        kc = _rms(proj[:, C_Q + c * LANES:C_Q + (c + 1) * LANES], kg_ref[...])
        k_ref[:, c * LANES:(c + 1) * LANES] = _rope(kc, cos, sin).astype(bf16)
        v_ref[:, 2 * c * LANES:(2 * c + 1) * LANES] = proj[:, C_Q + C_KV + c * LANES:C_Q + C_KV + (c + 1) * LANES].astype(bf16)
        v_ref[:, (2 * c + 1) * LANES:(2 * c + 2) * LANES] = jnp.ones((x_ref.shape[0], LANES), bf16)


def _odd_in(x2, g, w_bf, qg, kg, cos, sin, seq):
    n, d = x2.shape
    tm = ROW_TILE
    per_seq = seq // tm
    row = lambda i: (i, 0)
    tab = lambda i: (i % per_seq, 0)
    full = lambda i: (0, 0)
    return pl.pallas_call(
        _odd_in_kernel,
        grid=(n // tm,),
        in_specs=[pl.BlockSpec((tm, d), row), pl.BlockSpec((1, d), full), pl.BlockSpec(w_bf.shape, full),
                  pl.BlockSpec((1, HEAD_DIM_C), full), pl.BlockSpec((1, HEAD_DIM_C), full),
                  pl.BlockSpec((tm, LANES), tab), pl.BlockSpec((tm, LANES), tab)],
        out_specs=[pl.BlockSpec((tm, C_Q), row), pl.BlockSpec((tm, C_KV), row), pl.BlockSpec((tm, 2 * C_KV), row)],
        out_shape=[jax.ShapeDtypeStruct((n, C_Q), bf16), jax.ShapeDtypeStruct((n, C_KV), bf16),
                   jax.ShapeDtypeStruct((n, 2 * C_KV), bf16)],
        compiler_params=pltpu.CompilerParams(dimension_semantics=("arbitrary",)),
        name="odd_in_proj",
    )(x2, g, w_bf, qg, kg, cos, sin)


def _flash_kernel(q_ref, k_ref, v_ref, o_ref, m_sc, acc_sc):
    ki = pl.program_id(3)

    @pl.when(ki == 0)
    def _():
        m_sc[...] = jnp.full(m_sc.shape, NEG_INF, f32)
        acc_sc[...] = jnp.zeros(acc_sc.shape, f32)

    k = k_ref[...]
    v = v_ref[...]
    for g in range(GROUP_C):
        qg = q_ref[:, g * LANES:(g + 1) * LANES]
        s = lax.dot_general(qg, k, (((1,), (1,)), ((), ())), preferred_element_type=f32)
        m_prev = m_sc[g]
        m_new = jnp.maximum(m_prev, jnp.max(s, axis=1, keepdims=True))
        p = jnp.exp2(s - m_new).astype(bf16)
        acc_sc[g] = jnp.exp2(m_prev - m_new) * acc_sc[g] + jnp.dot(p, v, preferred_element_type=f32)
        m_sc[g] = m_new

    @pl.when(ki == pl.num_programs(3) - 1)
    def _():
        for g in range(GROUP_C):
            acc = acc_sc[g]
            o_ref[:, g * LANES:(g + 1) * LANES] = (acc[:, :HEAD_DIM_C] / acc[:, HEAD_DIM_C:]).astype(bf16)


def _flash(q3, k3, v3):
    b, seq, _ = q3.shape
    tq, tk = FLASH_TQ, min(FLASH_TK, seq)
    gw = GROUP_C * HEAD_DIM_C
    return pl.pallas_call(
        _flash_kernel,
        grid=(b, N_KV_C, seq // tq, seq // tk),
        in_specs=[pl.BlockSpec((None, tq, gw), lambda bi, j, qi, ki: (bi, qi, j)),
                  pl.BlockSpec((None, tk, HEAD_DIM_C), lambda bi, j, qi, ki: (bi, ki, j)),
                  pl.BlockSpec((None, tk, 2 * HEAD_DIM_C), lambda bi, j, qi, ki: (bi, ki, j))],
        out_specs=pl.BlockSpec((None, tq, gw), lambda bi, j, qi, ki: (bi, qi, j)),
        out_shape=jax.ShapeDtypeStruct((b, seq, C_Q), bf16),
        scratch_shapes=[pltpu.VMEM((GROUP_C, tq, 1), f32), pltpu.VMEM((GROUP_C, tq, 2 * HEAD_DIM_C), f32)],
        compiler_params=pltpu.CompilerParams(
            dimension_semantics=("arbitrary", "arbitrary", "arbitrary", "arbitrary")),
        name="dense_attention",
    )(q3, k3, v3)


def _odd_out_kernel(attn_ref, x_ref, wo_ref, g_ref, wr_ref, xo_ref, hpk_ref, aff_ref):
    proj = jnp.dot(attn_ref[...], wo_ref[...], preferred_element_type=f32)
    _post_mixer(x_ref[...] + proj, g_ref, wr_ref, xo_ref, hpk_ref, aff_ref)


def _odd_out(attn2, x2, wo_bf, g, wr, seq, b):
    n, d = x2.shape
    tm = ROW_TILE
    row = lambda i: (i, 0)
    full2 = lambda i: (0, 0)
    out_specs, out_shape = _post_specs(n, d, seq, tm, b)
    return pl.pallas_call(
        _odd_out_kernel,
        grid=(n // tm,),
        in_specs=[pl.BlockSpec((tm, C_Q), row), pl.BlockSpec((tm, d), row), pl.BlockSpec(wo_bf.shape, full2),
                  pl.BlockSpec((1, d), full2), pl.BlockSpec(wr.shape, full2)],
        out_specs=out_specs, out_shape=out_shape,
        compiler_params=pltpu.CompilerParams(dimension_semantics=("arbitrary",)),
        name="odd_out_proj",
    )(attn2, x2, wo_bf, g, wr)


def _topk_kernel(aff_ref, idx_ref, *, cap):
    ne, r, _ = aff_ref.shape
    er = ne * r
    shift = r.bit_length() - 1
    aff = aff_ref[...]

    def count(mask):
        return jnp.sum(jnp.sum(mask.astype(f32), axis=2, keepdims=True), axis=1, keepdims=True)

    def bit_step(it, bits):
        cand = bits | jnp.left_shift(jnp.int32(1), 30 - it)
        return jnp.where(count(aff >= lax.bitcast_convert_type(cand, f32)) >= cap, cand, bits)

    bits = lax.fori_loop(0, 31, bit_step, jnp.zeros((ne, 1, 1), i32))
    gt = aff >= lax.bitcast_convert_type(bits + 1, f32)
    eq = (aff >= lax.bitcast_convert_type(bits, f32)) & jnp.logical_not(gt)
    need = cap - count(gt)

    li = lax.broadcasted_iota(i32, (LANES, LANES), 0)
    lj = lax.broadcasted_iota(i32, (LANES, LANES), 1)
    ri = lax.broadcasted_iota(i32, (er, er), 0)
    rj = lax.broadcasted_iota(i32, (er, er), 1)
    rows_before = (((ri >> shift) == (rj >> shift)) & (rj < ri)).astype(bf16)

    ones_sq = jnp.ones((LANES, LANES), bf16)

    def row_stats(mask3, inclusive):
        x = mask3.astype(bf16).reshape(er, LANES)
        tri = ((li <= lj) if inclusive else (li < lj)).astype(bf16)
        within = jnp.dot(x, tri, preferred_element_type=f32)
        tot = jnp.dot(x, ones_sq, preferred_element_type=f32)
        before = jnp.dot(rows_before, tot.astype(bf16), preferred_element_type=f32)
        return within, tot, before

    w_eq, _, b_eq = row_stats(eq, False)
    sel = gt | (eq & ((w_eq + b_eq).reshape(ne, r, LANES) < need))
    within, tot, before = row_stats(sel, True)

    lane_f = lax.broadcasted_iota(i32, (r, LANES), 1).astype(f32)
    pad = LANES - r
    for e in range(ne):
        rows = slice(e * r, (e + 1) * r)
        pre, tt, we = before[rows], tot[rows], within[rows]
        inc = pre + tt
        if pad:
            we = jnp.concatenate([we, jnp.zeros((pad, LANES), f32)], axis=0)
        we_t = we.T.astype(bf16)
        row_of, rank, hot = [], [], []
        for c in range(cap // LANES):
            s = lane_f + float(c * LANES)
            done = inc <= s
            row_of.append(jnp.sum(done.astype(f32), axis=0, keepdims=True))
            rank.append(s[0:1, :] - jnp.sum(jnp.where(done, tt, 0.0), axis=0, keepdims=True))
            hot.append(((pre <= s) & jnp.logical_not(done)).astype(bf16))
        hot = jnp.concatenate(hot, axis=1)
        if pad:
            hot = jnp.concatenate([hot, jnp.zeros((pad, cap), bf16)], axis=0)
        run = jnp.dot(we_t, hot, preferred_element_type=f32)
        lane_of = jnp.sum((run <= jnp.concatenate(rank, axis=1)).astype(f32), axis=0, keepdims=True)
        idx_ref[e] = (jnp.concatenate(row_of, axis=1) * LANES + lane_of).astype(i32)


def _topk(aff4, cap):
    b, ne, r, _ = aff4.shape
    assert r & (r - 1) == 0 and r <= LANES and r % SUBLANES == 0 and cap % LANES == 0
    return pl.pallas_call(
        functools.partial(_topk_kernel, cap=cap),
        grid=(b,),
        in_specs=[pl.BlockSpec((None, ne, r, LANES), lambda bi: (bi, 0, 0, 0))],
        out_specs=pl.BlockSpec((None, ne, 1, cap), lambda bi: (bi, 0, 0, 0)),
        out_shape=jax.ShapeDtypeStruct((b, ne, 1, cap), i32),
        compiler_params=pltpu.CompilerParams(dimension_semantics=("arbitrary",)),
        name="expert_choice_topk",
    )(aff4)


def _ffn_kernel(idx_ref, idxn_ref, aff_ref, hpk_ref, x_ref, wg_ref, wu_ref, wd_ref, out_ref,
                acc, xsbuf, xs, yacc, gsem, xsem, *, seq, cap):
    b, e, j = pl.program_id(0), pl.program_id(1), pl.program_id(2)
    nb, ne, nj = pl.num_programs(0), pl.num_programs(1), pl.num_programs(2)
    n = b * ne + e
    slot = n % 2
    rows_per_step = cap // nj

    def gather_rows(iref, batch, dst_slot, lo, cnt):
        def trip(i, c):
            for k in range(ROW_UNROLL):
                s = lo + i * ROW_UNROLL + k
                pltpu.make_async_copy(hpk_ref.at[pl.ds(batch * seq + iref[0, s], 1), :],
                                      xsbuf.at[dst_slot, pl.ds(s, 1), :], gsem.at[dst_slot]).start()
            return c
        lax.fori_loop(0, cnt // ROW_UNROLL, trip, 0)

    def x_copy():
        return pltpu.make_async_copy(x_ref.at[pl.ds(b * seq, seq), :], acc, xsem.at[0])

    @pl.when((e == 0) & (j == 0))
    def _():
        x_copy().start()

    @pl.when((n == 0) & (j == 0))
    def _():
        gather_rows(idx_ref, b, slot, 0, cap)

    @pl.when(j == 0)
    def _():
        pltpu.make_async_copy(hpk_ref.at[pl.ds(0, cap), :], xsbuf.at[slot], gsem.at[slot]).wait()
        xs[...] = xsbuf[slot].astype(bf16)
        yacc[...] = jnp.zeros(yacc.shape, f32)

    @pl.when(n + 1 < nb * ne)
    def _():
        gather_rows(idxn_ref, (n + 1) // ne, 1 - slot, j * rows_per_step, rows_per_step)

    xv = xs[...]
    gate = jnp.dot(xv, wg_ref[...].astype(bf16), preferred_element_type=f32)
    up = jnp.dot(xv, wu_ref[...].astype(bf16), preferred_element_type=f32)
    hm = (gate * jax.nn.sigmoid(gate) * up).astype(bf16)
    yacc[...] += jnp.dot(hm, wd_ref[...].astype(bf16), preferred_element_type=f32)

    @pl.when(j == nj - 1)
    def _():
        @pl.when(e == 0)
        def _():
            x_copy().wait()

        def group(gi, c):
            s0 = gi * SCATTER_GROUP
            ts = [idx_ref[0, s0 + k] for k in range(SCATTER_GROUP)]
            new = [acc[pl.ds(ts[k], 1), :] + aff_ref[0, ts[k]] * yacc[pl.ds(s0 + k, 1), :]
                   for k in range(SCATTER_GROUP)]
            for k in range(SCATTER_GROUP):
                acc[pl.ds(ts[k], 1), :] = new[k]
            return c
        lax.fori_loop(0, cap // SCATTER_GROUP, group, 0)

        @pl.when(e == ne - 1)
        def _():
            cp = pltpu.make_async_copy(acc, out_ref.at[pl.ds(b * seq, seq), :], xsem.at[1])
            cp.start()
            cp.wait()


def _ffn(idx3, aff3, hpk, x2, wg, wu, wd, layer, b, seq, cap):
    n, d = x2.shape
    ne = N_EXPERTS
    ff = wg.shape[-1]
    tf = FF_TILE
    last = b * ne - 1
    cur = lambda bi, e, j: (bi * ne + e, 0, 0)
    nxt = lambda bi, e, j: (jnp.minimum(bi * ne + e + 1, last), 0, 0)
    smem = pltpu.SMEM
    return pl.pallas_call(
        functools.partial(_ffn_kernel, seq=seq, cap=cap),
        grid=(b, ne, ff // tf),
        in_specs=[pl.BlockSpec((None, 1, cap), cur, memory_space=smem),
                  pl.BlockSpec((None, 1, cap), nxt, memory_space=smem),
                  pl.BlockSpec((None, 1, seq), cur, memory_space=smem),
                  pl.BlockSpec(memory_space=pl.ANY), pl.BlockSpec(memory_space=pl.ANY),
                  pl.BlockSpec((None, None, d, tf), lambda bi, e, j: (layer, e, 0, j)),
                  pl.BlockSpec((None, None, d, tf), lambda bi, e, j: (layer, e, 0, j)),
                  pl.BlockSpec((None, None, tf, d), lambda bi, e, j: (layer, e, j, 0))],
        out_specs=pl.BlockSpec(memory_space=pl.ANY),
        out_shape=jax.ShapeDtypeStruct((n, d), f32),
        scratch_shapes=[pltpu.VMEM((seq, d), f32), pltpu.VMEM((2, cap, d), f32),
                        pltpu.VMEM((cap, d), bf16), pltpu.VMEM((cap, d), f32),
                        pltpu.SemaphoreType.DMA((2,)), pltpu.SemaphoreType.DMA((2,))],
        compiler_params=pltpu.CompilerParams(
            dimension_semantics=("arbitrary", "arbitrary", "arbitrary"),
            vmem_limit_bytes=VMEM_LIMIT_FFN),
        name="expert_ffn",
    )(idx3, idx3, aff3, hpk, x2, wg, wu, wd)


def _final_norm_kernel(x_ref, g_ref, o_ref):
    o_ref[...] = _rms(x_ref[...], g_ref[...])


def _final_norm(x2, g):
    n, d = x2.shape
    tm = ROW_TILE
    return pl.pallas_call(
        _final_norm_kernel,
        grid=(n // tm,),
        in_specs=[pl.BlockSpec((tm, d), lambda i: (i, 0)), pl.BlockSpec((1, d), lambda i: (0, 0))],
        out_specs=pl.BlockSpec((tm, d), lambda i: (i, 0)),
        out_shape=jax.ShapeDtypeStruct((n, d), f32),
        compiler_params=pltpu.CompilerParams(dimension_semantics=("arbitrary",)),
        name="final_norm",
    )(x2, g)


def kernel(x, norm_mix, norm_ffn, norm_final, a_w_in, a_w_out, a_sink, b_w_pool, b_scale,
           c_w_qkv, c_q_norm, c_k_norm, c_w_out, moe_router, moe_w_gate, moe_w_up, moe_w_down):
    b, seq, d = x.shape
    depth = norm_mix.shape[0]
    n = b * seq
    cap = CAPACITY_FACTOR * seq // N_EXPERTS
    assert seq % ROW_TILE == 0 and seq % FLASH_TQ == 0 and seq % min(FLASH_TK, seq) == 0 and seq % WIN_TQ == 0
    rows_per_step = cap // (moe_w_gate.shape[-1] // FF_TILE)
    assert rows_per_step % ROW_UNROLL == 0 and cap % SCATTER_GROUP == 0

    t = jnp.arange(seq, dtype=i32)
    cos_a, sin_a = _rope_tables(t, t, HEAD_DIM_A)
    cos_c, sin_c = _rope_tables(t // GRID_W, t % GRID_W, AXIAL_DIM)
    wr_pad = jnp.pad(moe_router, ((0, 0), (0, 0), (0, LANES - N_EXPERTS)))

    x2 = x.reshape(n, d)
    for i in range(depth):
        j = i // 2
        g_mix = norm_mix[i].reshape(1, d)
        g_ffn = norm_ffn[i].reshape(1, d)
        if i % 2 == 0:
            q, ka, kb, va, vb, u = _even_in(x2, g_mix, a_w_in[j].astype(bf16), cos_a, sin_a, seq)
            r3 = lambda a: a.reshape(b, seq, a.shape[-1])
            attn = _win_attn(a_sink[j], r3(q), r3(ka), r3(kb), r3(va), r3(vb))
            x2, hpk, aff = _even_out(attn.reshape(n, A_Q), u, x2, a_w_out[j].astype(bf16),
                                     b_w_pool[j].astype(bf16), b_scale[j].reshape(1, POOL_WIDTH),
                                     g_ffn, wr_pad[i], seq, b)
        else:
            q, k, v = _odd_in(x2, g_mix, c_w_qkv[j].astype(bf16), c_q_norm[j].reshape(1, HEAD_DIM_C),
                              c_k_norm[j].reshape(1, HEAD_DIM_C), cos_c, sin_c, seq)
            attn = _flash(q.reshape(b, seq, C_Q), k.reshape(b, seq, C_KV), v.reshape(b, seq, 2 * C_KV))
            x2, hpk, aff = _odd_out(attn.reshape(n, C_Q), x2, c_w_out[j].astype(bf16), g_ffn, wr_pad[i], seq, b)
        idx = _topk(aff.reshape(b, N_EXPERTS, seq // LANES, LANES), cap)
        x2 = _ffn(idx.reshape(b * N_EXPERTS, 1, cap), aff.reshape(b * N_EXPERTS, 1, seq), hpk, x2,
                  moe_w_gate, moe_w_up, moe_w_down, i, b, seq, cap)
    return _final_norm(x2, norm_final.reshape(1, d)).reshape(b, seq, d)
```

```python
import functools

import jax
import jax.numpy as jnp
from jax import lax
from jax.experimental import pallas as pl
from jax.experimental.pallas import tpu as pltpu

f32 = jnp.float32
bf16 = jnp.bfloat16
i32 = jnp.int32

ROPE_THETA = 10000.0
RMS_EPS = 1e-6
NEG_INF = -1e30
GRID_W = 64
N_HEADS_A, N_KV_A, HEAD_DIM_A = 8, 2, 64
WINDOW = 128
A_Q = N_HEADS_A * HEAD_DIM_A
A_KV = N_KV_A * HEAD_DIM_A
POOL_WINDOWS = (2, 4, 8, 16)
POOL_GROUP = 128
POOL_WIDTH = POOL_GROUP * len(POOL_WINDOWS)
N_HEADS_C, N_KV_C, HEAD_DIM_C = 8, 2, 128
GROUP_C = N_HEADS_C // N_KV_C
C_Q = N_HEADS_C * HEAD_DIM_C
C_KV = N_KV_C * HEAD_DIM_C
AXIAL_DIM = HEAD_DIM_C // 2
N_EXPERTS = 16
CAPACITY_FACTOR = 2

LANES = 128
SUBLANES = 8
ROW_TILE = 512
WIN_TQ = 256
FLASH_TQ = 512
FLASH_TK = 2048
LOG2_E = 1.4426950408889634
FF_TILE = 256
ROW_UNROLL = 8
SCATTER_GROUP = 4
VMEM_LIMIT_FFN = 60 * 1024 * 1024


def _rms(x, g):
    ms = jnp.mean(x * x, axis=-1, keepdims=True)
    return x * lax.rsqrt(ms + RMS_EPS) * g


def _load_rows(ref):
    t = ref.shape[0] // SUBLANES
    return jnp.concatenate([ref[pl.ds(c, t, stride=SUBLANES), :] for c in range(SUBLANES)], axis=1)


def _store_rows(ref, val):
    t = ref.shape[0] // SUBLANES
    for c in range(SUBLANES):
        ref[pl.ds(c, t, stride=SUBLANES), :] = val[:, c * LANES:(c + 1) * LANES]


def _rot_half32(x):
    lane = lax.broadcasted_iota(i32, x.shape, 1)
    first = (lane & 63) < 32
    return jnp.where(first, pltpu.roll(x, LANES - 32, 1), pltpu.roll(x, 32, 1))


def _rope(x, cos, sin_signed):
    return x * cos + _rot_half32(x) * sin_signed


def _rope_tables(pos1, pos2, dim):
    inv = ROPE_THETA ** (-jnp.arange(0, dim, 2, dtype=f32) / dim)

    def cs(pos):
        ang = pos.astype(f32)[:, None] * inv[None, :]
        return jnp.cos(ang), jnp.sin(ang)

    c1, s1 = cs(pos1)
    c2, s2 = cs(pos2)
    return (jnp.concatenate([c1, c1, c2, c2], axis=-1),
            jnp.concatenate([-s1, s1, -s2, s2], axis=-1))


def _tiled_spec(tm):
    return pl.BlockSpec((tm * SUBLANES, LANES), lambda i: (i, 0))


def _even_in_kernel(x_ref, g_ref, w_ref, cos_ref, sin_ref,
                    q_ref, ka_ref, kb_ref, va_ref, vb_ref, u_ref):
    h = _rms(_load_rows(x_ref), g_ref[...]).astype(bf16)
    proj = jnp.dot(h, w_ref[...], preferred_element_type=f32)
    cos = cos_ref[...]
    sin = sin_ref[...]
    scale = HEAD_DIM_A ** -0.5
    for c in range(A_Q // LANES):
        qc = proj[:, c * LANES:(c + 1) * LANES]
        q_ref[:, c * LANES:(c + 1) * LANES] = (_rope(qc, cos, sin) * scale).astype(bf16)
    kc = _rope(proj[:, A_Q:A_Q + A_KV], cos, sin)
    ka_ref[...] = kc.astype(bf16)
    kb_ref[...] = pltpu.roll(kc, HEAD_DIM_A, 1).astype(bf16)
    vc = proj[:, A_Q + A_KV:A_Q + 2 * A_KV]
    va_ref[...] = vc.astype(bf16)
    vb_ref[...] = pltpu.roll(vc, HEAD_DIM_A, 1).astype(bf16)
    u_ref[...] = proj[:, A_Q + 2 * A_KV:]


def _even_in(xt, g, w_bf, cos, sin, seq):
    n = xt.shape[0] // SUBLANES
    d = SUBLANES * LANES
    tm = ROW_TILE
    per_seq = seq // tm
    row = lambda i: (i, 0)
    tab = lambda i: (i % per_seq, 0)
    full = lambda i: (0, 0)
    return pl.pallas_call(
        _even_in_kernel,
        grid=(n // tm,),
        in_specs=[_tiled_spec(tm), pl.BlockSpec((1, d), full),
                  pl.BlockSpec(w_bf.shape, full),
                  pl.BlockSpec((tm, LANES), tab), pl.BlockSpec((tm, LANES), tab)],
        out_specs=[pl.BlockSpec((tm, A_Q), row)] + [pl.BlockSpec((tm, A_KV), row)] * 4
        + [pl.BlockSpec((tm, POOL_WIDTH), row)],
        out_shape=[jax.ShapeDtypeStruct((n, A_Q), bf16)] + [jax.ShapeDtypeStruct((n, A_KV), bf16)] * 4
        + [jax.ShapeDtypeStruct((n, POOL_WIDTH), f32)],
        compiler_params=pltpu.CompilerParams(dimension_semantics=("arbitrary",)),
        name="even_in_proj",
    )(xt, g, w_bf, cos, sin)


def _win_attn_kernel(sink_ref, q_ref, kap, kam, kan, kbp, kbm, kbn, vap, vam, van, vbp, vbm, vbn,
                     o_ref, *, seq):
    i = pl.program_id(1)
    tq = q_ref.shape[0]
    nk = tq + 2 * WINDOW
    cat = lambda p, m, n: jnp.concatenate([p[...], m[...], n[...]], axis=0)
    ka, kb, va, vb = cat(kap, kam, kan), cat(kbp, kbm, kbn), cat(vap, vam, van), cat(vbp, vbm, vbn)
    lo_k = lax.broadcasted_iota(i32, (nk, LANES), 1) < HEAD_DIM_A
    lo_q = lax.broadcasted_iota(i32, (tq, LANES), 1) < HEAD_DIM_A
    zero = jnp.zeros((nk, LANES), bf16)
    qpos = i * tq + lax.broadcasted_iota(i32, (tq, nk), 0)
    kpos = i * tq - WINDOW + lax.broadcasted_iota(i32, (tq, nk), 1)
    valid = (jnp.abs(qpos - kpos) <= WINDOW) & (kpos >= 0) & (kpos < seq)
    for c in range(A_Q // LANES):
        kv = c // (A_Q // LANES // N_KV_A)
        k_same, k_swap = (ka, kb) if kv == 0 else (kb, ka)
        v_same, v_swap = (va, vb) if kv == 0 else (vb, va)
        qc = q_ref[:, c * LANES:(c + 1) * LANES]
        halves = []
        for half in range(2):
            kz = jnp.where(lo_k, k_same, zero) if half == 0 else jnp.where(lo_k, zero, k_swap)
            vz = v_same if half == 0 else v_swap
            s = lax.dot_general(qc, kz, (((1,), (1,)), ((), ())), preferred_element_type=f32)
            s = jnp.where(valid, s, NEG_INF)
            sk = sink_ref[2 * c + half]
            m = jnp.maximum(jnp.max(s, axis=1, keepdims=True), sk)
            p = jnp.exp(s - m)
            l = jnp.sum(p, axis=1, keepdims=True) + jnp.exp(sk - m)
            pv = jnp.dot(p.astype(bf16), vz, preferred_element_type=f32)
            halves.append(pv / l)
        o_ref[:, c * LANES:(c + 1) * LANES] = jnp.where(lo_q, halves[0], halves[1]).astype(bf16)


def _win_attn(sink, q3, ka3, kb3, va3, vb3):
    b, seq, _ = q3.shape
    tq = WIN_TQ
    r = tq // WINDOW
    nwb = seq // WINDOW
    main = lambda bi, i: (bi, i, 0)
    prev = lambda bi, i: (bi, jnp.maximum(i * r - 1, 0), 0)
    nxt = lambda bi, i: (bi, jnp.minimum((i + 1) * r, nwb - 1), 0)
    kv_specs = [pl.BlockSpec((None, WINDOW, A_KV), prev), pl.BlockSpec((None, tq, A_KV), main),
                pl.BlockSpec((None, WINDOW, A_KV), nxt)]
    return pl.pallas_call(
        functools.partial(_win_attn_kernel, seq=seq),
        grid=(b, seq // tq),
        in_specs=[pl.BlockSpec(memory_space=pltpu.SMEM), pl.BlockSpec((None, tq, A_Q), main)] + kv_specs * 4,
        out_specs=pl.BlockSpec((None, tq, A_Q), main),
        out_shape=jax.ShapeDtypeStruct((b, seq, A_Q), bf16),
        compiler_params=pltpu.CompilerParams(dimension_semantics=("arbitrary", "arbitrary")),
        name="window_attention",
    )(sink, q3, ka3, ka3, ka3, kb3, kb3, kb3, va3, va3, va3, vb3, vb3, vb3)


def _post_mixer(xn, g_ref, wr_ref, xo_ref, h_ref, aff_ref):
    _store_rows(xo_ref, xn)
    h2 = _rms(xn, g_ref[...])
    _store_rows(h_ref, h2)
    wr = wr_ref[...]
    h_hi, w_hi = h2.astype(bf16), wr.astype(bf16)
    h_lo, w_lo = (h2 - h_hi.astype(f32)).astype(bf16), (wr - w_hi.astype(f32)).astype(bf16)
    logits = (jnp.dot(h_hi, w_hi, preferred_element_type=f32) + jnp.dot(h_lo, w_hi, preferred_element_type=f32)
              + jnp.dot(h_hi, w_lo, preferred_element_type=f32))
    lane = lax.broadcasted_iota(i32, logits.shape, 1)
    logits = jnp.where(lane < N_EXPERTS, logits, NEG_INF)
    e = jnp.exp(logits - jnp.max(logits, axis=1, keepdims=True))
    aff = e / jnp.sum(e, axis=1, keepdims=True)
    aff_ref[...] = aff.T[:N_EXPERTS, :]


def _post_specs(n, seq, tm, b):
    per_seq = seq // tm
    out_specs = [_tiled_spec(tm), _tiled_spec(tm),
                 pl.BlockSpec((None, N_EXPERTS, tm), lambda i: (i // per_seq, 0, i % per_seq))]
    out_shape = [jax.ShapeDtypeStruct((n * SUBLANES, LANES), f32), jax.ShapeDtypeStruct((n * SUBLANES, LANES), f32),
                 jax.ShapeDtypeStruct((b, N_EXPERTS, seq), f32)]
    return out_specs, out_shape


def _even_out_kernel(attn_ref, um_ref, up_ref, un_ref, x_ref, wo_ref, wp_ref, sc_ref, g_ref, wr_ref,
                     xo_ref, h_ref, aff_ref, ubuf, *, seq):
    tm = attn_ref.shape[0]
    halo = SUBLANES
    pos0 = (pl.program_id(0) % (seq // tm)) * tm
    um = um_ref[...]
    ubuf[0:halo, :] = jnp.where(pos0 > 0, up_ref[...], 0.0)
    ubuf[halo:halo + tm, :] = um
    ubuf[halo + tm:, :] = jnp.where(pos0 + tm < seq, un_ref[...], 0.0)
    t = pos0 + lax.broadcasted_iota(i32, (tm, 1), 0)
    proj = jnp.dot(attn_ref[...], wo_ref[0:A_Q, :], preferred_element_type=f32)
    for g, w in enumerate(POOL_WINDOWS):
        hw = w // 2
        cols = slice(g * POOL_GROUP, (g + 1) * POOL_GROUP)
        acc = ubuf[halo - hw:halo - hw + tm, cols]
        for o in range(-hw + 1, hw):
            acc = acc + ubuf[halo + o:halo + o + tm, cols]
        cnt = (jnp.minimum(t + hw, seq) - jnp.maximum(t - hw, 0)).astype(f32)
        mixed = acc / cnt - um[:, cols]
        y = jnp.dot(mixed.astype(bf16), wp_ref[g], preferred_element_type=f32) * sc_ref[:, cols]
        proj = proj + jnp.dot(y.astype(bf16), wo_ref[A_Q + g * POOL_GROUP:A_Q + (g + 1) * POOL_GROUP, :],
                              preferred_element_type=f32)
    _post_mixer(_load_rows(x_ref) + proj, g_ref, wr_ref, xo_ref, h_ref, aff_ref)


def _even_out(attn2, u2, xt, wo_bf, wp_bf, scale, g, wr, seq, b):
    n = xt.shape[0] // SUBLANES
    d = SUBLANES * LANES
    tm = ROW_TILE
    halo = SUBLANES
    row = lambda i: (i, 0)
    full2 = lambda i: (0, 0)
    nhb = n // halo
    out_specs, out_shape = _post_specs(n, seq, tm, b)
    return pl.pallas_call(
        functools.partial(_even_out_kernel, seq=seq),
        grid=(n // tm,),
        in_specs=[pl.BlockSpec((tm, A_Q), row), pl.BlockSpec((tm, POOL_WIDTH), row),
                  pl.BlockSpec((halo, POOL_WIDTH), lambda i: (jnp.maximum(i * (tm // halo) - 1, 0), 0)),
                  pl.BlockSpec((halo, POOL_WIDTH), lambda i: (jnp.minimum((i + 1) * (tm // halo), nhb - 1), 0)),
                  _tiled_spec(tm), pl.BlockSpec(wo_bf.shape, full2),
                  pl.BlockSpec(wp_bf.shape, lambda i: (0, 0, 0)), pl.BlockSpec((1, POOL_WIDTH), full2),
                  pl.BlockSpec((1, d), full2), pl.BlockSpec(wr.shape, full2)],
        out_specs=out_specs, out_shape=out_shape,
        scratch_shapes=[pltpu.VMEM((tm + 2 * halo, POOL_WIDTH), f32)],
        compiler_params=pltpu.CompilerParams(dimension_semantics=("arbitrary",)),
        name="even_out_proj",
    )(attn2, u2, u2, u2, xt, wo_bf, wp_bf, scale, g, wr)


def _odd_in_kernel(x_ref, g_ref, w_ref, qg_ref, kg_ref, cos_ref, sin_ref, q_ref, k_ref, v_ref):
    h = _rms(_load_rows(x_ref), g_ref[...]).astype(bf16)
    proj = jnp.dot(h, w_ref[...], preferred_element_type=f32)
    cos = cos_ref[...]
    sin = sin_ref[...]
    scale = HEAD_DIM_C ** -0.5 * LOG2_E
    for c in range(N_HEADS_C):
        qc = _rms(proj[:, c * LANES:(c + 1) * LANES], qg_ref[...])
        q_ref[:, c * LANES:(c + 1) * LANES] = (_rope(qc, cos, sin) * scale).astype(bf16)
    for c in range(N_KV_C):
        kc = _rms(proj[:, C_Q + c * LANES:C_Q + (c + 1) * LANES], kg_ref[...])
        k_ref[:, c * LANES:(c + 1) * LANES] = _rope(kc, cos, sin).astype(bf16)
        v_ref[:, 2 * c * LANES:(2 * c + 1) * LANES] = proj[:, C_Q + C_KV + c * LANES:C_Q + C_KV + (c + 1) * LANES].astype(bf16)
        v_ref[:, (2 * c + 1) * LANES:(2 * c + 2) * LANES] = jnp.ones((v_ref.shape[0], LANES), bf16)


def _odd_in(xt, g, w_bf, qg, kg, cos, sin, seq):
    n = xt.shape[0] // SUBLANES
    d = SUBLANES * LANES
    tm = ROW_TILE
    per_seq = seq // tm
    row = lambda i: (i, 0)
    tab = lambda i: (i % per_seq, 0)
    full = lambda i: (0, 0)
    return pl.pallas_call(
        _odd_in_kernel,
        grid=(n // tm,),
        in_specs=[_tiled_spec(tm), pl.BlockSpec((1, d), full), pl.BlockSpec(w_bf.shape, full),
                  pl.BlockSpec((1, HEAD_DIM_C), full), pl.BlockSpec((1, HEAD_DIM_C), full),
                  pl.BlockSpec((tm, LANES), tab), pl.BlockSpec((tm, LANES), tab)],
        out_specs=[pl.BlockSpec((tm, C_Q), row), pl.BlockSpec((tm, C_KV), row), pl.BlockSpec((tm, 2 * C_KV), row)],
        out_shape=[jax.ShapeDtypeStruct((n, C_Q), bf16), jax.ShapeDtypeStruct((n, C_KV), bf16),
                   jax.ShapeDtypeStruct((n, 2 * C_KV), bf16)],
        compiler_params=pltpu.CompilerParams(dimension_semantics=("arbitrary",)),
        name="odd_in_proj",
    )(xt, g, w_bf, qg, kg, cos, sin)


def _flash_kernel(q_ref, k_ref, v_ref, o_ref, m_sc, acc_sc):
    ki = pl.program_id(3)

    @pl.when(ki == 0)
    def _():
        m_sc[...] = jnp.full(m_sc.shape, NEG_INF, f32)
        acc_sc[...] = jnp.zeros(acc_sc.shape, f32)

    k = k_ref[...]
    v = v_ref[...]
    for g in range(GROUP_C):
        qg = q_ref[:, g * LANES:(g + 1) * LANES]
        s = lax.dot_general(qg, k, (((1,), (1,)), ((), ())), preferred_element_type=f32)
        m_prev = m_sc[g]
        m_new = jnp.maximum(m_prev, jnp.max(s, axis=1, keepdims=True))
        p = jnp.exp2(s - m_new).astype(bf16)
        acc_sc[g] = jnp.exp2(m_prev - m_new) * acc_sc[g] + jnp.dot(p, v, preferred_element_type=f32)
        m_sc[g] = m_new

    @pl.when(ki == pl.num_programs(3) - 1)
    def _():
        for g in range(GROUP_C):
            acc = acc_sc[g]
            o_ref[:, g * LANES:(g + 1) * LANES] = (acc[:, :HEAD_DIM_C] / acc[:, HEAD_DIM_C:]).astype(bf16)


def _flash(q3, k3, v3):
    b, seq, _ = q3.shape
    tq, tk = FLASH_TQ, min(FLASH_TK, seq)
    gw = GROUP_C * HEAD_DIM_C
    return pl.pallas_call(
        _flash_kernel,
        grid=(b, N_KV_C, seq // tq, seq // tk),
        in_specs=[pl.BlockSpec((None, tq, gw), lambda bi, j, qi, ki: (bi, qi, j)),
                  pl.BlockSpec((None, tk, HEAD_DIM_C), lambda bi, j, qi, ki: (bi, ki, j)),
                  pl.BlockSpec((None, tk, 2 * HEAD_DIM_C), lambda bi, j, qi, ki: (bi, ki, j))],
        out_specs=pl.BlockSpec((None, tq, gw), lambda bi, j, qi, ki: (bi, qi, j)),
        out_shape=jax.ShapeDtypeStruct((b, seq, C_Q), bf16),
        scratch_shapes=[pltpu.VMEM((GROUP_C, tq, 1), f32), pltpu.VMEM((GROUP_C, tq, 2 * HEAD_DIM_C), f32)],
        compiler_params=pltpu.CompilerParams(
            dimension_semantics=("arbitrary", "arbitrary", "arbitrary", "arbitrary")),
        name="dense_attention",
    )(q3, k3, v3)


def _odd_out_kernel(attn_ref, x_ref, wo_ref, g_ref, wr_ref, xo_ref, h_ref, aff_ref):
    proj = jnp.dot(attn_ref[...], wo_ref[...], preferred_element_type=f32)
    _post_mixer(_load_rows(x_ref) + proj, g_ref, wr_ref, xo_ref, h_ref, aff_ref)


def _odd_out(attn2, xt, wo_bf, g, wr, seq, b):
    n = xt.shape[0] // SUBLANES
    d = SUBLANES * LANES
    tm = ROW_TILE
    row = lambda i: (i, 0)
    full2 = lambda i: (0, 0)
    out_specs, out_shape = _post_specs(n, seq, tm, b)
    return pl.pallas_call(
        _odd_out_kernel,
        grid=(n // tm,),
        in_specs=[pl.BlockSpec((tm, C_Q), row), _tiled_spec(tm), pl.BlockSpec(wo_bf.shape, full2),
                  pl.BlockSpec((1, d), full2), pl.BlockSpec(wr.shape, full2)],
        out_specs=out_specs, out_shape=out_shape,
        compiler_params=pltpu.CompilerParams(dimension_semantics=("arbitrary",)),
        name="odd_out_proj",
    )(attn2, xt, wo_bf, g, wr)


def _topk_kernel(aff_ref, idx_ref, *, cap):
    ne, r, _ = aff_ref.shape
    er = ne * r
    shift = r.bit_length() - 1
    aff = aff_ref[...]

    def count(mask):
        return jnp.sum(jnp.sum(mask.astype(f32), axis=2, keepdims=True), axis=1, keepdims=True)

    def bit_step(it, bits):
        cand = bits | jnp.left_shift(jnp.int32(1), 30 - it)
        return jnp.where(count(aff >= lax.bitcast_convert_type(cand, f32)) >= cap, cand, bits)

    bits = lax.fori_loop(0, 31, bit_step, jnp.zeros((ne, 1, 1), i32))
    gt = aff >= lax.bitcast_convert_type(bits + 1, f32)
    eq = (aff >= lax.bitcast_convert_type(bits, f32)) & jnp.logical_not(gt)
    need = cap - count(gt)

    li = lax.broadcasted_iota(i32, (LANES, LANES), 0)
    lj = lax.broadcasted_iota(i32, (LANES, LANES), 1)
    ri = lax.broadcasted_iota(i32, (er, er), 0)
    rj = lax.broadcasted_iota(i32, (er, er), 1)
    rows_before = (((ri >> shift) == (rj >> shift)) & (rj < ri)).astype(bf16)
    ones_sq = jnp.ones((LANES, LANES), bf16)

    def row_stats(mask3, inclusive):
        x = mask3.astype(bf16).reshape(er, LANES)
        tri = ((li <= lj) if inclusive else (li < lj)).astype(bf16)
        within = jnp.dot(x, tri, preferred_element_type=f32)
        tot = jnp.dot(x, ones_sq, preferred_element_type=f32)
        before = jnp.dot(rows_before, tot.astype(bf16), preferred_element_type=f32)
        return within, tot, before

    w_eq, _, b_eq = row_stats(eq, False)
    sel = gt | (eq & ((w_eq + b_eq).reshape(ne, r, LANES) < need))
    within, tot, before = row_stats(sel, True)

    lane_f = lax.broadcasted_iota(i32, (r, LANES), 1).astype(f32)
    pad = LANES - r
    for e in range(ne):
        rows = slice(e * r, (e + 1) * r)
        pre, tt, we = before[rows], tot[rows], within[rows]
        inc = pre + tt
        if pad:
            we = jnp.concatenate([we, jnp.zeros((pad, LANES), f32)], axis=0)
        we_t = we.T.astype(bf16)
        row_of, rank, hot = [], [], []
        for c in range(cap // LANES):
            s = lane_f + float(c * LANES)
            done = inc <= s
            row_of.append(jnp.sum(done.astype(f32), axis=0, keepdims=True))
            rank.append(s[0:1, :] - jnp.sum(jnp.where(done, tt, 0.0), axis=0, keepdims=True))
            hot.append(((pre <= s) & jnp.logical_not(done)).astype(bf16))
        hot = jnp.concatenate(hot, axis=1)
        if pad:
            hot = jnp.concatenate([hot, jnp.zeros((pad, cap), bf16)], axis=0)
        run = jnp.dot(we_t, hot, preferred_element_type=f32)
        lane_of = jnp.sum((run <= jnp.concatenate(rank, axis=1)).astype(f32), axis=0, keepdims=True)
        idx_ref[e] = (jnp.concatenate(row_of, axis=1) * LANES + lane_of).astype(i32)


def _topk(aff4, cap):
    b, ne, r, _ = aff4.shape
    assert r & (r - 1) == 0 and r <= LANES and r % SUBLANES == 0 and cap % LANES == 0
    return pl.pallas_call(
        functools.partial(_topk_kernel, cap=cap),
        grid=(b,),
        in_specs=[pl.BlockSpec((None, ne, r, LANES), lambda bi: (bi, 0, 0, 0))],
        out_specs=pl.BlockSpec((None, ne, 1, cap), lambda bi: (bi, 0, 0, 0)),
        out_shape=jax.ShapeDtypeStruct((b, ne, 1, cap), i32),
        compiler_params=pltpu.CompilerParams(dimension_semantics=("arbitrary",)),
        name="expert_choice_topk",
    )(aff4)


def _ffn_kernel(idx_ref, idxn_ref, idxp_ref, affp_ref, h_ref, x_ref, wg_ref, wu_ref, wd_ref, out_ref,
                acc, xsbuf, xs, yacc, ydone, gsem, xsem, *, seq, cap, ne):
    n, j = pl.program_id(0), pl.program_id(1)
    n_exp, nj = pl.num_programs(0) - 1, pl.num_programs(1)
    slot = n % 2
    tile = SUBLANES
    rows_per_step = cap // nj
    batch_cur = jnp.minimum(n, n_exp - 1) // ne
    batch_next = jnp.minimum(n + 1, n_exp - 1) // ne
    batch_prev = jnp.maximum(n - 1, 0) // ne

    def token(ref, t):
        return ref.at[pl.ds(pl.multiple_of(t * tile, tile), tile), :]

    def gather_start(token_row, s, buf_slot):
        pltpu.make_async_copy(token(h_ref, token_row), token(xsbuf.at[buf_slot], s), gsem.at[buf_slot]).start()

    def gather_wait(buf_slot):
        pltpu.make_async_copy(h_ref.at[pl.ds(0, cap * tile), :], xsbuf.at[buf_slot], gsem.at[buf_slot]).wait()

    def acc_load(batch):
        cp = pltpu.make_async_copy(x_ref.at[pl.ds(batch * seq * tile, seq * tile), :], acc, xsem.at[0])
        cp.start()
        cp.wait()

    def scatter_prev_rows():
        for g0 in range(0, rows_per_step, SCATTER_GROUP):
            ss = [j * rows_per_step + g0 + k for k in range(SCATTER_GROUP)]
            ts = [idxp_ref[0, s] for s in ss]
            new = [token(acc, t)[...] + affp_ref[0, t] * token(ydone, s)[...] for s, t in zip(ss, ts)]
            for t, v in zip(ts, new):
                token(acc, t)[...] = v

    @pl.when((n == 0) & (j == 0))
    def _():
        def trip(i, c):
            for k in range(ROW_UNROLL):
                s = i * ROW_UNROLL + k
                gather_start(idx_ref[0, s], s, 0)
            return c
        lax.fori_loop(0, cap // ROW_UNROLL, trip, 0)
        ydone[...] = jnp.zeros(ydone.shape, f32)
        acc_load(0)

    @pl.when((j == 0) & (n < n_exp))
    def _():
        gather_wait(slot)
        xs[...] = _load_rows(xsbuf.at[slot]).astype(bf16)
        yacc[...] = jnp.zeros(yacc.shape, f32)

    @pl.when(n < n_exp)
    def _():
        for k in range(rows_per_step):
            s = j * rows_per_step + k
            gather_start(batch_next * seq + idxn_ref[0, s], s, 1 - slot)
        xv = xs[...]
        gate = jnp.dot(xv, wg_ref[...].astype(bf16), preferred_element_type=f32)
        up = jnp.dot(xv, wu_ref[...].astype(bf16), preferred_element_type=f32)
        hm = (gate * jax.nn.sigmoid(gate) * up).astype(bf16)
        yacc[...] += jnp.dot(hm, wd_ref[...].astype(bf16), preferred_element_type=f32)
        scatter_prev_rows()

    @pl.when(n == n_exp)
    def _():
        scatter_prev_rows()

    @pl.when((j == nj - 1) & (n < n_exp))
    def _():
        _store_rows(ydone, yacc[...])

    @pl.when((j == nj - 1) & (n == n_exp - 1))
    def _():
        gather_wait(1 - slot)

    @pl.when((j == nj - 1) & (n > 0) & (n % ne == 0))
    def _():
        cp = pltpu.make_async_copy(acc, out_ref.at[pl.ds(batch_prev * seq * tile, seq * tile), :], xsem.at[1])
        cp.start()
        cp.wait()

        @pl.when(n < n_exp)
        def _():
            acc_load(batch_cur)


def _ffn(idx3, aff3, ht, xt, wg, wu, wd, layer, b, seq, cap):
    n = xt.shape[0] // SUBLANES
    d = SUBLANES * LANES
    ne = N_EXPERTS
    n_exp = b * ne
    tf = FF_TILE
    nj = wg.shape[-1] // tf
    cur = lambda i, j: (jnp.minimum(i, n_exp - 1), 0, 0)
    nxt = lambda i, j: (jnp.minimum(i + 1, n_exp - 1), 0, 0)
    prv = lambda i, j: (jnp.maximum(i - 1, 0), 0, 0)
    expert = lambda i: jnp.minimum(i, n_exp - 1) % ne
    col = lambda i, j: jnp.where(i < n_exp, j, nj - 1)
    smem = pltpu.SMEM
    return pl.pallas_call(
        functools.partial(_ffn_kernel, seq=seq, cap=cap, ne=ne),
        grid=(n_exp + 1, nj),
        in_specs=[pl.BlockSpec((None, 1, cap), cur, memory_space=smem),
                  pl.BlockSpec((None, 1, cap), nxt, memory_space=smem),
                  pl.BlockSpec((None, 1, cap), prv, memory_space=smem),
                  pl.BlockSpec((None, 1, seq), prv, memory_space=smem),
                  pl.BlockSpec(memory_space=pl.ANY), pl.BlockSpec(memory_space=pl.ANY),
                  pl.BlockSpec((None, None, d, tf), lambda i, j: (layer, expert(i), 0, col(i, j))),
                  pl.BlockSpec((None, None, d, tf), lambda i, j: (layer, expert(i), 0, col(i, j))),
                  pl.BlockSpec((None, None, tf, d), lambda i, j: (layer, expert(i), col(i, j), 0))],
        out_specs=pl.BlockSpec(memory_space=pl.ANY),
        out_shape=jax.ShapeDtypeStruct((n * SUBLANES, LANES), f32),
        scratch_shapes=[pltpu.VMEM((seq * SUBLANES, LANES), f32), pltpu.VMEM((2, cap * SUBLANES, LANES), f32),
                        pltpu.VMEM((cap, d), bf16), pltpu.VMEM((cap, d), f32),
                        pltpu.VMEM((cap * SUBLANES, LANES), f32),
                        pltpu.SemaphoreType.DMA((2,)), pltpu.SemaphoreType.DMA((2,))],
        compiler_params=pltpu.CompilerParams(
            dimension_semantics=("arbitrary", "arbitrary"),
            vmem_limit_bytes=VMEM_LIMIT_FFN),
        name="expert_ffn",
    )(idx3, idx3, idx3, aff3, ht, xt, wg, wu, wd)


def _final_norm_kernel(x_ref, g_ref, o_ref):
    o_ref[...] = _rms(_load_rows(x_ref), g_ref[...])


def _final_norm(xt, g):
    n = xt.shape[0] // SUBLANES
    d = SUBLANES * LANES
    tm = ROW_TILE
    return pl.pallas_call(
        _final_norm_kernel,
        grid=(n // tm,),
        in_specs=[_tiled_spec(tm), pl.BlockSpec((1, d), lambda i: (0, 0))],
        out_specs=pl.BlockSpec((tm, d), lambda i: (i, 0)),
        out_shape=jax.ShapeDtypeStruct((n, d), f32),
        compiler_params=pltpu.CompilerParams(dimension_semantics=("arbitrary",)),
        name="final_norm",
    )(xt, g)


def kernel(x, norm_mix, norm_ffn, norm_final, a_w_in, a_w_out, a_sink, b_w_pool, b_scale,
           c_w_qkv, c_q_norm, c_k_norm, c_w_out, moe_router, moe_w_gate, moe_w_up, moe_w_down):
    b, seq, d = x.shape
    depth = norm_mix.shape[0]
    n = b * seq
    cap = CAPACITY_FACTOR * seq // N_EXPERTS
    assert d == SUBLANES * LANES
    assert seq % ROW_TILE == 0 and seq % FLASH_TQ == 0 and seq % min(FLASH_TK, seq) == 0 and seq % WIN_TQ == 0
    rows_per_step = cap // (moe_w_gate.shape[-1] // FF_TILE)
    assert rows_per_step % SCATTER_GROUP == 0 and cap % ROW_UNROLL == 0

    t = jnp.arange(seq, dtype=i32)
    cos_a, sin_a = _rope_tables(t, t, HEAD_DIM_A)
    cos_c, sin_c = _rope_tables(t // GRID_W, t % GRID_W, AXIAL_DIM)
    wr_pad = jnp.pad(moe_router, ((0, 0), (0, 0), (0, LANES - N_EXPERTS)))

    xt = x.reshape(n * SUBLANES, LANES)
    for i in range(depth):
        j = i // 2
        g_mix = norm_mix[i].reshape(1, d)
        g_ffn = norm_ffn[i].reshape(1, d)
        if i % 2 == 0:
            q, ka, kb, va, vb, u = _even_in(xt, g_mix, a_w_in[j].astype(bf16), cos_a, sin_a, seq)
            r3 = lambda a: a.reshape(b, seq, a.shape[-1])
            attn = _win_attn(a_sink[j], r3(q), r3(ka), r3(kb), r3(va), r3(vb))
            xt, ht, aff = _even_out(attn.reshape(n, A_Q), u, xt, a_w_out[j].astype(bf16),
                                    b_w_pool[j].astype(bf16), b_scale[j].reshape(1, POOL_WIDTH),
                                    g_ffn, wr_pad[i], seq, b)
        else:
            q, k, v = _odd_in(xt, g_mix, c_w_qkv[j].astype(bf16), c_q_norm[j].reshape(1, HEAD_DIM_C),
                              c_k_norm[j].reshape(1, HEAD_DIM_C), cos_c, sin_c, seq)
            attn = _flash(q.reshape(b, seq, C_Q), k.reshape(b, seq, C_KV), v.reshape(b, seq, 2 * C_KV))
            xt, ht, aff = _odd_out(attn.reshape(n, C_Q), xt, c_w_out[j].astype(bf16), g_ffn, wr_pad[i], seq, b)
        idx = _topk(aff.reshape(b, N_EXPERTS, seq // LANES, LANES), cap)
        xt = _ffn(idx.reshape(b * N_EXPERTS, 1, cap), aff.reshape(b * N_EXPERTS, 1, seq), ht, xt,
                  moe_w_gate, moe_w_up, moe_w_down, i, b, seq, cap)
    return _final_norm(xt, norm_final.reshape(1, d)).reshape(b, seq, d)
```

```python
import functools

import jax
import jax.numpy as jnp
from jax import lax
from jax.experimental import pallas as pl
from jax.experimental.pallas import tpu as pltpu

f32 = jnp.float32
bf16 = jnp.bfloat16
i32 = jnp.int32

ROPE_THETA = 10000.0
RMS_EPS = 1e-6
NEG_INF = -1e30
GRID_W = 64
N_HEADS_A, N_KV_A, HEAD_DIM_A = 8, 2, 64
WINDOW = 128
A_Q = N_HEADS_A * HEAD_DIM_A
A_KV = N_KV_A * HEAD_DIM_A
POOL_WINDOWS = (2, 4, 8, 16)
POOL_GROUP = 128
POOL_WIDTH = POOL_GROUP * len(POOL_WINDOWS)
N_HEADS_C, N_KV_C, HEAD_DIM_C = 8, 2, 128
GROUP_C = N_HEADS_C // N_KV_C
C_Q = N_HEADS_C * HEAD_DIM_C
C_KV = N_KV_C * HEAD_DIM_C
AXIAL_DIM = HEAD_DIM_C // 2
N_EXPERTS = 16
CAPACITY_FACTOR = 2

LANES = 128
SUBLANES = 8
ROW_TILE = 512
WIN_TQ = 256
FLASH_TQ = 1024
FLASH_TK = 2048
FLASH_ROW_SPLIT = 4
LOG2_E = 1.4426950408889634
FF_TILE = 256
ROW_UNROLL = 8
SCATTER_GROUP = 4
VMEM_LIMIT_FFN = 60 * 1024 * 1024


def _rms(x, g):
    ms = jnp.mean(x * x, axis=-1, keepdims=True)
    return x * lax.rsqrt(ms + RMS_EPS) * g


def _load_rows(ref):
    if ref.shape[1] != LANES:
        return ref[...]
    t = ref.shape[0] // SUBLANES
    return jnp.concatenate([ref[pl.ds(c, t, stride=SUBLANES), :] for c in range(SUBLANES)], axis=1)


def _store_rows(ref, val):
    t = ref.shape[0] // SUBLANES
    for c in range(SUBLANES):
        ref[pl.ds(c, t, stride=SUBLANES), :] = val[:, c * LANES:(c + 1) * LANES]


def _rot_half32(x):
    lane = lax.broadcasted_iota(i32, x.shape, 1)
    first = (lane & 63) < 32
    return jnp.where(first, pltpu.roll(x, LANES - 32, 1), pltpu.roll(x, 32, 1))


def _rope(x, cos, sin_signed):
    return x * cos + _rot_half32(x) * sin_signed


def _rope_tables(pos1, pos2, dim):
    inv = ROPE_THETA ** (-jnp.arange(0, dim, 2, dtype=f32) / dim)

    def cs(pos):
        ang = pos.astype(f32)[:, None] * inv[None, :]
        return jnp.cos(ang), jnp.sin(ang)

    c1, s1 = cs(pos1)
    c2, s2 = cs(pos2)
    return (jnp.concatenate([c1, c1, c2, c2], axis=-1),
            jnp.concatenate([-s1, s1, -s2, s2], axis=-1))


def _tiled_spec(tm):
    return pl.BlockSpec((tm * SUBLANES, LANES), lambda i: (i, 0))


def _rows_spec(x, tm):
    return _tiled_spec(tm) if x.shape[1] == LANES else pl.BlockSpec((tm, x.shape[1]), lambda i: (i, 0))


def _num_tokens(x):
    return x.shape[0] // SUBLANES if x.shape[1] == LANES else x.shape[0]


def _even_in_kernel(x_ref, g_ref, w_ref, cos_ref, sin_ref,
                    q_ref, ka_ref, kb_ref, va_ref, vb_ref, u_ref):
    h = _rms(_load_rows(x_ref), g_ref[...]).astype(bf16)
    proj = jnp.dot(h, w_ref[...], preferred_element_type=f32)
    cos = cos_ref[...]
    sin = sin_ref[...]
    scale = HEAD_DIM_A ** -0.5 * LOG2_E
    for c in range(A_Q // LANES):
        qc = proj[:, c * LANES:(c + 1) * LANES]
        q_ref[:, c * LANES:(c + 1) * LANES] = (_rope(qc, cos, sin) * scale).astype(bf16)
    kc = _rope(proj[:, A_Q:A_Q + A_KV], cos, sin)
    ka_ref[...] = kc.astype(bf16)
    kb_ref[...] = pltpu.roll(kc, HEAD_DIM_A, 1).astype(bf16)
    vc = proj[:, A_Q + A_KV:A_Q + 2 * A_KV]
    va_ref[...] = vc.astype(bf16)
    vb_ref[...] = pltpu.roll(vc, HEAD_DIM_A, 1).astype(bf16)
    u_ref[...] = proj[:, A_Q + 2 * A_KV:]


def _even_in(xt, g, w_bf, cos, sin, seq):
    n = _num_tokens(xt)
    d = SUBLANES * LANES
    tm = ROW_TILE
    per_seq = seq // tm
    row = lambda i: (i, 0)
    tab = lambda i: (i % per_seq, 0)
    full = lambda i: (0, 0)
    return pl.pallas_call(
        _even_in_kernel,
        grid=(n // tm,),
        in_specs=[_rows_spec(xt, tm), pl.BlockSpec((1, d), full),
                  pl.BlockSpec(w_bf.shape, full),
                  pl.BlockSpec((tm, LANES), tab), pl.BlockSpec((tm, LANES), tab)],
        out_specs=[pl.BlockSpec((tm, A_Q), row)] + [pl.BlockSpec((tm, A_KV), row)] * 4
        + [pl.BlockSpec((tm, POOL_WIDTH), row)],
        out_shape=[jax.ShapeDtypeStruct((n, A_Q), bf16)] + [jax.ShapeDtypeStruct((n, A_KV), bf16)] * 4
        + [jax.ShapeDtypeStruct((n, POOL_WIDTH), f32)],
        compiler_params=pltpu.CompilerParams(dimension_semantics=("arbitrary",)),
        name="even_in_proj",
    )(xt, g, w_bf, cos, sin)


def _win_attn_kernel(sink_ref, q_ref, kap, kam, kan, kbp, kbm, kbn, vap, vam, van, vbp, vbm, vbn,
                     o_ref, *, seq):
    i = pl.program_id(1)
    tq = q_ref.shape[0]
    nk = tq + 2 * WINDOW
    cat = lambda p, m, n: jnp.concatenate([p[...], m[...], n[...]], axis=0)
    ka, kb, va, vb = cat(kap, kam, kan), cat(kbp, kbm, kbn), cat(vap, vam, van), cat(vbp, vbm, vbn)
    lo_k = lax.broadcasted_iota(i32, (nk, LANES), 1) < HEAD_DIM_A
    lo_q = lax.broadcasted_iota(i32, (tq, LANES), 1) < HEAD_DIM_A
    zero = jnp.zeros((nk, LANES), bf16)
    ones = jnp.ones((nk, LANES), bf16)
    qpos = i * tq + lax.broadcasted_iota(i32, (tq, nk), 0)
    kpos = i * tq - WINDOW + lax.broadcasted_iota(i32, (tq, nk), 1)
    valid = (jnp.abs(qpos - kpos) <= WINDOW) & (kpos >= 0) & (kpos < seq)
    bias = jnp.where(valid, 0.0, NEG_INF)
    k_lo = {0: jnp.where(lo_k, ka, zero), 1: jnp.where(lo_k, kb, zero)}
    k_hi = {0: jnp.where(lo_k, zero, kb), 1: jnp.where(lo_k, zero, ka)}
    v_lo = {0: jnp.concatenate([va, ones], axis=1), 1: jnp.concatenate([vb, ones], axis=1)}
    v_hi = {0: v_lo[1], 1: v_lo[0]}
    for c in range(A_Q // LANES):
        kv = c // (A_Q // LANES // N_KV_A)
        qc = q_ref[:, c * LANES:(c + 1) * LANES]
        halves = []
        for half in range(2):
            kz, vz = (k_lo[kv], v_lo[kv]) if half == 0 else (k_hi[kv], v_hi[kv])
            s = lax.dot_general(qc, kz, (((1,), (1,)), ((), ())), preferred_element_type=f32) + bias
            sk = sink_ref[2 * c + half] * LOG2_E
            m = jnp.maximum(jnp.max(s, axis=1, keepdims=True), sk)
            pv = jnp.dot(jnp.exp2(s - m).astype(bf16), vz, preferred_element_type=f32)
            halves.append(pv[:, :LANES] / (pv[:, LANES:] + jnp.exp2(sk - m)))
        o_ref[:, c * LANES:(c + 1) * LANES] = jnp.where(lo_q, halves[0], halves[1]).astype(bf16)


def _win_attn(sink, q3, ka3, kb3, va3, vb3):
    b, seq, _ = q3.shape
    tq = WIN_TQ
    r = tq // WINDOW
    nwb = seq // WINDOW
    main = lambda bi, i: (bi, i, 0)
    prev = lambda bi, i: (bi, jnp.maximum(i * r - 1, 0), 0)
    nxt = lambda bi, i: (bi, jnp.minimum((i + 1) * r, nwb - 1), 0)
    kv_specs = [pl.BlockSpec((None, WINDOW, A_KV), prev), pl.BlockSpec((None, tq, A_KV), main),
                pl.BlockSpec((None, WINDOW, A_KV), nxt)]
    return pl.pallas_call(
        functools.partial(_win_attn_kernel, seq=seq),
        grid=(b, seq // tq),
        in_specs=[pl.BlockSpec(memory_space=pltpu.SMEM), pl.BlockSpec((None, tq, A_Q), main)] + kv_specs * 4,
        out_specs=pl.BlockSpec((None, tq, A_Q), main),
        out_shape=jax.ShapeDtypeStruct((b, seq, A_Q), bf16),
        compiler_params=pltpu.CompilerParams(dimension_semantics=("arbitrary", "arbitrary")),
        name="window_attention",
    )(sink, q3, ka3, ka3, ka3, kb3, kb3, kb3, va3, va3, va3, vb3, vb3, vb3)


def _post_mixer(xn, g_ref, wr_ref, xo_ref, h_ref, aff_ref):
    _store_rows(xo_ref, xn)
    h2 = _rms(xn, g_ref[...])
    _store_rows(h_ref, h2)
    wr = wr_ref[...]
    h_hi, w_hi = h2.astype(bf16), wr.astype(bf16)
    h_lo, w_lo = (h2 - h_hi.astype(f32)).astype(bf16), (wr - w_hi.astype(f32)).astype(bf16)
    logits = (jnp.dot(h_hi, w_hi, preferred_element_type=f32) + jnp.dot(h_lo, w_hi, preferred_element_type=f32)
              + jnp.dot(h_hi, w_lo, preferred_element_type=f32))
    lane = lax.broadcasted_iota(i32, logits.shape, 1)
    logits = jnp.where(lane < N_EXPERTS, logits, NEG_INF)
    e = jnp.exp(logits - jnp.max(logits, axis=1, keepdims=True))
    aff = e / jnp.sum(e, axis=1, keepdims=True)
    aff_ref[...] = aff.T[:N_EXPERTS, :]


def _post_specs(n, seq, tm, b):
    per_seq = seq // tm
    out_specs = [_tiled_spec(tm), _tiled_spec(tm),
                 pl.BlockSpec((None, N_EXPERTS, tm), lambda i: (i // per_seq, 0, i % per_seq))]
    out_shape = [jax.ShapeDtypeStruct((n * SUBLANES, LANES), f32), jax.ShapeDtypeStruct((n * SUBLANES, LANES), f32),
                 jax.ShapeDtypeStruct((b, N_EXPERTS, seq), f32)]
    return out_specs, out_shape


def _even_out_kernel(attn_ref, um_ref, up_ref, un_ref, x_ref, wo_ref, wp_ref, sc_ref, g_ref, wr_ref,
                     xo_ref, h_ref, aff_ref, ubuf, *, seq):
    tm = attn_ref.shape[0]
    halo = SUBLANES
    pos0 = (pl.program_id(0) % (seq // tm)) * tm
    um = um_ref[...]
    ubuf[0:halo, :] = jnp.where(pos0 > 0, up_ref[...], 0.0)
    ubuf[halo:halo + tm, :] = um
    ubuf[halo + tm:, :] = jnp.where(pos0 + tm < seq, un_ref[...], 0.0)
    t = pos0 + lax.broadcasted_iota(i32, (tm, 1), 0)
    proj = jnp.dot(attn_ref[...], wo_ref[0:A_Q, :], preferred_element_type=f32)
    for g, w in enumerate(POOL_WINDOWS):
        hw = w // 2
        cols = slice(g * POOL_GROUP, (g + 1) * POOL_GROUP)
        acc = ubuf[halo - hw:halo - hw + tm, cols]
        for o in range(-hw + 1, hw):
            acc = acc + ubuf[halo + o:halo + o + tm, cols]
        cnt = (jnp.minimum(t + hw, seq) - jnp.maximum(t - hw, 0)).astype(f32)
        mixed = acc / cnt - um[:, cols]
        y = jnp.dot(mixed.astype(bf16), wp_ref[g], preferred_element_type=f32) * sc_ref[:, cols]
        proj = proj + jnp.dot(y.astype(bf16), wo_ref[A_Q + g * POOL_GROUP:A_Q + (g + 1) * POOL_GROUP, :],
                              preferred_element_type=f32)
    _post_mixer(_load_rows(x_ref) + proj, g_ref, wr_ref, xo_ref, h_ref, aff_ref)


def _even_out(attn2, u2, xt, wo_bf, wp_bf, scale, g, wr, seq, b):
    n = _num_tokens(xt)
    d = SUBLANES * LANES
    tm = ROW_TILE
    halo = SUBLANES
    row = lambda i: (i, 0)
    full2 = lambda i: (0, 0)
    nhb = n // halo
    out_specs, out_shape = _post_specs(n, seq, tm, b)
    return pl.pallas_call(
        functools.partial(_even_out_kernel, seq=seq),
        grid=(n // tm,),
        in_specs=[pl.BlockSpec((tm, A_Q), row), pl.BlockSpec((tm, POOL_WIDTH), row),
                  pl.BlockSpec((halo, POOL_WIDTH), lambda i: (jnp.maximum(i * (tm // halo) - 1, 0), 0)),
                  pl.BlockSpec((halo, POOL_WIDTH), lambda i: (jnp.minimum((i + 1) * (tm // halo), nhb - 1), 0)),
                  _rows_spec(xt, tm), pl.BlockSpec(wo_bf.shape, full2),
                  pl.BlockSpec(wp_bf.shape, lambda i: (0, 0, 0)), pl.BlockSpec((1, POOL_WIDTH), full2),
                  pl.BlockSpec((1, d), full2), pl.BlockSpec(wr.shape, full2)],
        out_specs=out_specs, out_shape=out_shape,
        scratch_shapes=[pltpu.VMEM((tm + 2 * halo, POOL_WIDTH), f32)],
        compiler_params=pltpu.CompilerParams(dimension_semantics=("arbitrary",)),
        name="even_out_proj",
    )(attn2, u2, u2, u2, xt, wo_bf, wp_bf, scale, g, wr)


def _odd_in_kernel(x_ref, g_ref, w_ref, qg_ref, kg_ref, cos_ref, sin_ref, q_ref, k_ref, v_ref):
    h = _rms(_load_rows(x_ref), g_ref[...]).astype(bf16)
    proj = jnp.dot(h, w_ref[...], preferred_element_type=f32)
    cos = cos_ref[...]
    sin = sin_ref[...]
    scale = HEAD_DIM_C ** -0.5 * LOG2_E
    for c in range(N_HEADS_C):
        qc = _rms(proj[:, c * LANES:(c + 1) * LANES], qg_ref[...])
        q_ref[:, c * LANES:(c + 1) * LANES] = (_rope(qc, cos, sin) * scale).astype(bf16)
    for c in range(N_KV_C):
        kc = _rms(proj[:, C_Q + c * LANES:C_Q + (c + 1) * LANES], kg_ref[...])
        k_ref[:, c * LANES:(c + 1) * LANES] = _rope(kc, cos, sin).astype(bf16)
        v_ref[:, 2 * c * LANES:(2 * c + 1) * LANES] = proj[:, C_Q + C_KV + c * LANES:C_Q + C_KV + (c + 1) * LANES].astype(bf16)
        v_ref[:, (2 * c + 1) * LANES:(2 * c + 2) * LANES] = jnp.ones((v_ref.shape[0], LANES), bf16)


def _odd_in(xt, g, w_bf, qg, kg, cos, sin, seq):
    n = _num_tokens(xt)
    d = SUBLANES * LANES
    tm = ROW_TILE
    per_seq = seq // tm
    row = lambda i: (i, 0)
    tab = lambda i: (i % per_seq, 0)
    full = lambda i: (0, 0)
    return pl.pallas_call(
        _odd_in_kernel,
        grid=(n // tm,),
        in_specs=[_tiled_spec(tm), pl.BlockSpec((1, d), full), pl.BlockSpec(w_bf.shape, full),
                  pl.BlockSpec((1, HEAD_DIM_C), full), pl.BlockSpec((1, HEAD_DIM_C), full),
                  pl.BlockSpec((tm, LANES), tab), pl.BlockSpec((tm, LANES), tab)],
        out_specs=[pl.BlockSpec((tm, C_Q), row), pl.BlockSpec((tm, C_KV), row), pl.BlockSpec((tm, 2 * C_KV), row)],
        out_shape=[jax.ShapeDtypeStruct((n, C_Q), bf16), jax.ShapeDtypeStruct((n, C_KV), bf16),
                   jax.ShapeDtypeStruct((n, 2 * C_KV), bf16)],
        compiler_params=pltpu.CompilerParams(dimension_semantics=("arbitrary",)),
        name="odd_in_proj",
    )(xt, g, w_bf, qg, kg, cos, sin)


def _flash_kernel(q_ref, k_ref, v_ref, o_ref, m_sc, acc_sc):
    ki = pl.program_id(3)

    @pl.when(ki == 0)
    def _():
        m_sc[...] = jnp.full(m_sc.shape, NEG_INF, f32)
        acc_sc[...] = jnp.zeros(acc_sc.shape, f32)

    k = k_ref[...]
    v = v_ref[...]
    tq = q_ref.shape[0]
    rows_per_unit = tq // FLASH_ROW_SPLIT
    for g in range(GROUP_C):
        for r0 in range(0, tq, rows_per_unit):
            rows = slice(r0, r0 + rows_per_unit)
            qg = q_ref[rows, g * LANES:(g + 1) * LANES]
            s = lax.dot_general(qg, k, (((1,), (1,)), ((), ())), preferred_element_type=f32)
            m_prev = m_sc[g, rows]
            m_new = jnp.maximum(m_prev, jnp.max(s, axis=1, keepdims=True))
            p = jnp.exp2(s - m_new).astype(bf16)
            acc_sc[g, rows] = (jnp.exp2(m_prev - m_new) * acc_sc[g, rows]
                               + jnp.dot(p, v, preferred_element_type=f32))
            m_sc[g, rows] = m_new

    @pl.when(ki == pl.num_programs(3) - 1)
    def _():
        for g in range(GROUP_C):
            acc = acc_sc[g]
            o_ref[:, g * LANES:(g + 1) * LANES] = (acc[:, :HEAD_DIM_C] / acc[:, HEAD_DIM_C:]).astype(bf16)


def _flash(q3, k3, v3):
    b, seq, _ = q3.shape
    tq, tk = FLASH_TQ, min(FLASH_TK, seq)
    gw = GROUP_C * HEAD_DIM_C
    return pl.pallas_call(
        _flash_kernel,
        grid=(b, N_KV_C, seq // tq, seq // tk),
        in_specs=[pl.BlockSpec((None, tq, gw), lambda bi, j, qi, ki: (bi, qi, j)),
                  pl.BlockSpec((None, tk, HEAD_DIM_C), lambda bi, j, qi, ki: (bi, ki, j)),
                  pl.BlockSpec((None, tk, 2 * HEAD_DIM_C), lambda bi, j, qi, ki: (bi, ki, j))],
        out_specs=pl.BlockSpec((None, tq, gw), lambda bi, j, qi, ki: (bi, qi, j)),
        out_shape=jax.ShapeDtypeStruct((b, seq, C_Q), bf16),
        scratch_shapes=[pltpu.VMEM((GROUP_C, tq, 1), f32), pltpu.VMEM((GROUP_C, tq, 2 * HEAD_DIM_C), f32)],
        compiler_params=pltpu.CompilerParams(
            dimension_semantics=("arbitrary", "arbitrary", "arbitrary", "arbitrary")),
        name="dense_attention",
    )(q3, k3, v3)


def _odd_out_kernel(attn_ref, x_ref, wo_ref, g_ref, wr_ref, xo_ref, h_ref, aff_ref):
    proj = jnp.dot(attn_ref[...], wo_ref[...], preferred_element_type=f32)
    _post_mixer(_load_rows(x_ref) + proj, g_ref, wr_ref, xo_ref, h_ref, aff_ref)


def _odd_out(attn2, xt, wo_bf, g, wr, seq, b):
    n = _num_tokens(xt)
    d = SUBLANES * LANES
    tm = ROW_TILE
    row = lambda i: (i, 0)
    full2 = lambda i: (0, 0)
    out_specs, out_shape = _post_specs(n, seq, tm, b)
    return pl.pallas_call(
        _odd_out_kernel,
        grid=(n // tm,),
        in_specs=[pl.BlockSpec((tm, C_Q), row), _tiled_spec(tm), pl.BlockSpec(wo_bf.shape, full2),
                  pl.BlockSpec((1, d), full2), pl.BlockSpec(wr.shape, full2)],
        out_specs=out_specs, out_shape=out_shape,
        compiler_params=pltpu.CompilerParams(dimension_semantics=("arbitrary",)),
        name="odd_out_proj",
    )(attn2, xt, wo_bf, g, wr)


def _topk_kernel(aff_ref, idx_ref, *, cap):
    ne, r, _ = aff_ref.shape
    er = ne * r
    shift = r.bit_length() - 1
    aff = aff_ref[...]

    def count(mask):
        return jnp.sum(jnp.sum(mask.astype(f32), axis=2, keepdims=True), axis=1, keepdims=True)

    def bit_step(it, bits):
        cand = bits | jnp.left_shift(jnp.int32(1), 30 - it)
        return jnp.where(count(aff >= lax.bitcast_convert_type(cand, f32)) >= cap, cand, bits)

    bits = lax.fori_loop(0, 31, bit_step, jnp.zeros((ne, 1, 1), i32))
    gt = aff >= lax.bitcast_convert_type(bits + 1, f32)
    eq = (aff >= lax.bitcast_convert_type(bits, f32)) & jnp.logical_not(gt)
    need = cap - count(gt)

    li = lax.broadcasted_iota(i32, (LANES, LANES), 0)
    lj = lax.broadcasted_iota(i32, (LANES, LANES), 1)
    ri = lax.broadcasted_iota(i32, (er, er), 0)
    rj = lax.broadcasted_iota(i32, (er, er), 1)
    rows_before = (((ri >> shift) == (rj >> shift)) & (rj < ri)).astype(bf16)
    ones_sq = jnp.ones((LANES, LANES), bf16)

    def row_stats(mask3, inclusive):
        x = mask3.astype(bf16).reshape(er, LANES)
        tri = ((li <= lj) if inclusive else (li < lj)).astype(bf16)
        within = jnp.dot(x, tri, preferred_element_type=f32)
        tot = jnp.dot(x, ones_sq, preferred_element_type=f32)
        before = jnp.dot(rows_before, tot.astype(bf16), preferred_element_type=f32)
        return within, tot, before

    w_eq, _, b_eq = row_stats(eq, False)
    sel = gt | (eq & ((w_eq + b_eq).reshape(ne, r, LANES) < need))
    within, tot, before = row_stats(sel, True)

    lane_f = lax.broadcasted_iota(i32, (r, LANES), 1).astype(f32)
    pad = LANES - r
    for e in range(ne):
        rows = slice(e * r, (e + 1) * r)
        pre, tt, we = before[rows], tot[rows], within[rows]
        inc = pre + tt
        if pad:
            we = jnp.concatenate([we, jnp.zeros((pad, LANES), f32)], axis=0)
        we_t = we.T.astype(bf16)
        row_of, rank, hot = [], [], []
        for c in range(cap // LANES):
            s = lane_f + float(c * LANES)
            done = inc <= s
            row_of.append(jnp.sum(done.astype(f32), axis=0, keepdims=True))
            rank.append(s[0:1, :] - jnp.sum(jnp.where(done, tt, 0.0), axis=0, keepdims=True))
            hot.append(((pre <= s) & jnp.logical_not(done)).astype(bf16))
        hot = jnp.concatenate(hot, axis=1)
        if pad:
            hot = jnp.concatenate([hot, jnp.zeros((pad, cap), bf16)], axis=0)
        run = jnp.dot(we_t, hot, preferred_element_type=f32)
        lane_of = jnp.sum((run <= jnp.concatenate(rank, axis=1)).astype(f32), axis=0, keepdims=True)
        idx_ref[e] = (jnp.concatenate(row_of, axis=1) * LANES + lane_of).astype(i32)


def _topk(aff4, cap):
    b, ne, r, _ = aff4.shape
    assert r & (r - 1) == 0 and r <= LANES and r % SUBLANES == 0 and cap % LANES == 0
    return pl.pallas_call(
        functools.partial(_topk_kernel, cap=cap),
        grid=(b,),
        in_specs=[pl.BlockSpec((None, ne, r, LANES), lambda bi: (bi, 0, 0, 0))],
        out_specs=pl.BlockSpec((None, ne, 1, cap), lambda bi: (bi, 0, 0, 0)),
        out_shape=jax.ShapeDtypeStruct((b, ne, 1, cap), i32),
        compiler_params=pltpu.CompilerParams(dimension_semantics=("arbitrary",)),
        name="expert_choice_topk",
    )(aff4)


def _ffn_kernel(idx_ref, idxn_ref, idxp_ref, affp_ref, h_ref, x_ref, wg_ref, wu_ref, wd_ref, out_ref,
                acc, xsbuf, xs, yacc, ydone, gsem, xsem, *, seq, cap, ne):
    n, j = pl.program_id(0), pl.program_id(1)
    n_exp, nj = pl.num_programs(0) - 1, pl.num_programs(1)
    tile = SUBLANES
    rows_per_step = cap // nj
    batch_cur = jnp.minimum(n, n_exp - 1) // ne
    batch_next = jnp.minimum(n + 1, n_exp - 1) // ne
    batch_prev = jnp.maximum(n - 1, 0) // ne

    def token(ref, t):
        return ref.at[pl.ds(pl.multiple_of(t * tile, tile), tile), :]

    def gather_start(token_row, s):
        pltpu.make_async_copy(token(h_ref, token_row), token(xsbuf, s), gsem.at[0]).start()

    def gather_wait():
        pltpu.make_async_copy(h_ref.at[pl.ds(0, cap * tile), :], xsbuf, gsem.at[0]).wait()

    def acc_load(batch):
        cp = pltpu.make_async_copy(x_ref.at[pl.ds(batch * seq * tile, seq * tile), :], acc, xsem.at[0])
        cp.start()
        cp.wait()

    def scatter_prev_rows():
        for g0 in range(0, rows_per_step, SCATTER_GROUP):
            ss = [j * rows_per_step + g0 + k for k in range(SCATTER_GROUP)]
            ts = [idxp_ref[0, s] for s in ss]
            new = [token(acc, t)[...] + affp_ref[0, t] * token(ydone, s)[...] for s, t in zip(ss, ts)]
            for t, v in zip(ts, new):
                token(acc, t)[...] = v

    @pl.when((n == 0) & (j == 0))
    def _():
        def trip(i, c):
            for k in range(ROW_UNROLL):
                s = i * ROW_UNROLL + k
                gather_start(idx_ref[0, s], s)
            return c
        lax.fori_loop(0, cap // ROW_UNROLL, trip, 0)
        ydone[...] = jnp.zeros(ydone.shape, f32)
        acc_load(0)

    @pl.when((j == 0) & (n < n_exp))
    def _():
        gather_wait()
        xs[...] = _load_rows(xsbuf).astype(bf16)
        yacc[...] = jnp.zeros(yacc.shape, f32)

    @pl.when(n < n_exp)
    def _():
        for k in range(rows_per_step):
            s = j * rows_per_step + k
            gather_start(batch_next * seq + idxn_ref[0, s], s)
        xv = xs[...]
        gate = jnp.dot(xv, wg_ref[...].astype(bf16), preferred_element_type=f32)
        up = jnp.dot(xv, wu_ref[...].astype(bf16), preferred_element_type=f32)
        hm = (gate * jax.nn.sigmoid(gate) * up).astype(bf16)
        yacc[...] += jnp.dot(hm, wd_ref[...].astype(bf16), preferred_element_type=f32)
        scatter_prev_rows()

    @pl.when(n == n_exp)
    def _():
        scatter_prev_rows()

    @pl.when((j == nj - 1) & (n < n_exp))
    def _():
        _store_rows(ydone, yacc[...])

    @pl.when((j == nj - 1) & (n == n_exp - 1))
    def _():
        gather_wait()

    @pl.when((j == nj - 1) & (n > 0) & (n % ne == 0))
    def _():
        cp = pltpu.make_async_copy(acc, out_ref.at[pl.ds(batch_prev * seq * tile, seq * tile), :], xsem.at[1])
        cp.start()
        cp.wait()

        @pl.when(n < n_exp)
        def _():
            acc_load(batch_cur)


def _ffn(idx3, aff3, ht, xt, wg, wu, wd, layer, b, seq, cap):
    n = _num_tokens(xt)
    d = SUBLANES * LANES
    ne = N_EXPERTS
    n_exp = b * ne
    tf = FF_TILE
    nj = wg.shape[-1] // tf
    cur = lambda i, j: (jnp.minimum(i, n_exp - 1), 0, 0)
    nxt = lambda i, j: (jnp.minimum(i + 1, n_exp - 1), 0, 0)
    prv = lambda i, j: (jnp.maximum(i - 1, 0), 0, 0)
    expert = lambda i: jnp.minimum(i, n_exp - 1) % ne
    col = lambda i, j: jnp.where(i < n_exp, j, nj - 1)
    smem = pltpu.SMEM
    return pl.pallas_call(
        functools.partial(_ffn_kernel, seq=seq, cap=cap, ne=ne),
        grid=(n_exp + 1, nj),
        in_specs=[pl.BlockSpec((None, 1, cap), cur, memory_space=smem),
                  pl.BlockSpec((None, 1, cap), nxt, memory_space=smem),
                  pl.BlockSpec((None, 1, cap), prv, memory_space=smem),
                  pl.BlockSpec((None, 1, seq), prv, memory_space=smem),
                  pl.BlockSpec(memory_space=pl.ANY), pl.BlockSpec(memory_space=pl.ANY),
                  pl.BlockSpec((None, None, d, tf), lambda i, j: (layer, expert(i), 0, col(i, j))),
                  pl.BlockSpec((None, None, d, tf), lambda i, j: (layer, expert(i), 0, col(i, j))),
                  pl.BlockSpec((None, None, tf, d), lambda i, j: (layer, expert(i), col(i, j), 0))],
        out_specs=pl.BlockSpec(memory_space=pl.ANY),
        out_shape=jax.ShapeDtypeStruct((n * SUBLANES, LANES), f32),
        scratch_shapes=[pltpu.VMEM((seq * SUBLANES, LANES), f32), pltpu.VMEM((cap * SUBLANES, LANES), f32),
                        pltpu.VMEM((cap, d), bf16), pltpu.VMEM((cap, d), f32),
                        pltpu.VMEM((cap * SUBLANES, LANES), f32),
                        pltpu.SemaphoreType.DMA((1,)), pltpu.SemaphoreType.DMA((2,))],
        compiler_params=pltpu.CompilerParams(
            dimension_semantics=("arbitrary", "arbitrary"),
            vmem_limit_bytes=VMEM_LIMIT_FFN),
        name="expert_ffn",
    )(idx3, idx3, idx3, aff3, ht, xt, wg, wu, wd)


def _final_norm_kernel(x_ref, g_ref, o_ref):
    o_ref[...] = _rms(_load_rows(x_ref), g_ref[...])


def _final_norm(xt, g):
    n = _num_tokens(xt)
    d = SUBLANES * LANES
    tm = ROW_TILE
    return pl.pallas_call(
        _final_norm_kernel,
        grid=(n // tm,),
        in_specs=[_tiled_spec(tm), pl.BlockSpec((1, d), lambda i: (0, 0))],
        out_specs=pl.BlockSpec((tm, d), lambda i: (i, 0)),
        out_shape=jax.ShapeDtypeStruct((n, d), f32),
        compiler_params=pltpu.CompilerParams(dimension_semantics=("arbitrary",)),
        name="final_norm",
    )(xt, g)


def kernel(x, norm_mix, norm_ffn, norm_final, a_w_in, a_w_out, a_sink, b_w_pool, b_scale,
           c_w_qkv, c_q_norm, c_k_norm, c_w_out, moe_router, moe_w_gate, moe_w_up, moe_w_down):
    b, seq, d = x.shape
    depth = norm_mix.shape[0]
    n = b * seq
    cap = CAPACITY_FACTOR * seq // N_EXPERTS
    assert d == SUBLANES * LANES
    assert seq % ROW_TILE == 0 and seq % FLASH_TQ == 0 and seq % min(FLASH_TK, seq) == 0 and seq % WIN_TQ == 0
    rows_per_step = cap // (moe_w_gate.shape[-1] // FF_TILE)
    assert rows_per_step % SCATTER_GROUP == 0 and cap % ROW_UNROLL == 0

    t = jnp.arange(seq, dtype=i32)
    cos_a, sin_a = _rope_tables(t, t, HEAD_DIM_A)
    cos_c, sin_c = _rope_tables(t // GRID_W, t % GRID_W, AXIAL_DIM)
    wr_pad = jnp.pad(moe_router, ((0, 0), (0, 0), (0, LANES - N_EXPERTS)))

    xt = x.reshape(n, d)
    for i in range(depth):
        j = i // 2
        g_mix = norm_mix[i].reshape(1, d)
        g_ffn = norm_ffn[i].reshape(1, d)
        if i % 2 == 0:
            q, ka, kb, va, vb, u = _even_in(xt, g_mix, a_w_in[j].astype(bf16), cos_a, sin_a, seq)
            r3 = lambda a: a.reshape(b, seq, a.shape[-1])
            attn = _win_attn(a_sink[j], r3(q), r3(ka), r3(kb), r3(va), r3(vb))
            xt, ht, aff = _even_out(attn.reshape(n, A_Q), u, xt, a_w_out[j].astype(bf16),
                                    b_w_pool[j].astype(bf16), b_scale[j].reshape(1, POOL_WIDTH),
                                    g_ffn, wr_pad[i], seq, b)
        else:
            q, k, v = _odd_in(xt, g_mix, c_w_qkv[j].astype(bf16), c_q_norm[j].reshape(1, HEAD_DIM_C),
                              c_k_norm[j].reshape(1, HEAD_DIM_C), cos_c, sin_c, seq)
            attn = _flash(q.reshape(b, seq, C_Q), k.reshape(b, seq, C_KV), v.reshape(b, seq, 2 * C_KV))
            xt, ht, aff = _odd_out(attn.reshape(n, C_Q), xt, c_w_out[j].astype(bf16), g_ffn, wr_pad[i], seq, b)
        idx = _topk(aff.reshape(b, N_EXPERTS, seq // LANES, LANES), cap)
        xt = _ffn(idx.reshape(b * N_EXPERTS, 1, cap), aff.reshape(b * N_EXPERTS, 1, seq), ht, xt,
                  moe_w_gate, moe_w_up, moe_w_down, i, b, seq, cap)
    return _final_norm(xt, norm_final.reshape(1, d)).reshape(b, seq, d)
```

```python
import functools

import jax
import jax.numpy as jnp
from jax import lax
from jax.experimental import pallas as pl
from jax.experimental.pallas import tpu as pltpu

f32 = jnp.float32
bf16 = jnp.bfloat16
i32 = jnp.int32

ROPE_THETA = 10000.0
RMS_EPS = 1e-6
NEG_INF = -1e30
GRID_W = 64
N_HEADS_A, N_KV_A, HEAD_DIM_A = 8, 2, 64
WINDOW = 128
A_Q = N_HEADS_A * HEAD_DIM_A
A_KV = N_KV_A * HEAD_DIM_A
POOL_WINDOWS = (2, 4, 8, 16)
POOL_GROUP = 128
POOL_WIDTH = POOL_GROUP * len(POOL_WINDOWS)
N_HEADS_C, N_KV_C, HEAD_DIM_C = 8, 2, 128
GROUP_C = N_HEADS_C // N_KV_C
C_Q = N_HEADS_C * HEAD_DIM_C
C_KV = N_KV_C * HEAD_DIM_C
AXIAL_DIM = HEAD_DIM_C // 2
N_EXPERTS = 16
CAPACITY_FACTOR = 2

LANES = 128
SUBLANES = 8
ROW_TILE = 512
ROW_UNITS = 2
WIN_TQ = 256
FLASH_TQ = 1024
FLASH_TK = 4096
FLASH_ROW_SPLIT = 4
LOG2_E = 1.4426950408889634
FF_TILE = 256
ROW_UNROLL = 8
SCATTER_GROUP = 4
VMEM_LIMIT_FFN = 60 * 1024 * 1024


def _rms(x, g):
    ms = jnp.mean(x * x, axis=-1, keepdims=True)
    return x * lax.rsqrt(ms + RMS_EPS) * g


def _load_rows(ref, r0=0, rows=None):
    if ref.shape[1] != LANES:
        return ref[...] if rows is None else ref[r0:r0 + rows, :]
    rows = ref.shape[0] // SUBLANES if rows is None else rows
    return jnp.concatenate([ref[pl.ds(r0 * SUBLANES + c, rows, stride=SUBLANES), :] for c in range(SUBLANES)],
                           axis=1)


def _store_rows(ref, val, r0=0):
    for c in range(SUBLANES):
        ref[pl.ds(r0 * SUBLANES + c, val.shape[0], stride=SUBLANES), :] = val[:, c * LANES:(c + 1) * LANES]


def _rot_half32(x):
    lane = lax.broadcasted_iota(i32, x.shape, 1)
    first = (lane & 63) < 32
    return jnp.where(first, pltpu.roll(x, LANES - 32, 1), pltpu.roll(x, 32, 1))


def _rope(x, cos, sin_signed):
    return x * cos + _rot_half32(x) * sin_signed


def _rope_tables(pos1, pos2, dim):
    inv = ROPE_THETA ** (-jnp.arange(0, dim, 2, dtype=f32) / dim)

    def cs(pos):
        ang = pos.astype(f32)[:, None] * inv[None, :]
        return jnp.cos(ang), jnp.sin(ang)

    c1, s1 = cs(pos1)
    c2, s2 = cs(pos2)
    return (jnp.concatenate([c1, c1, c2, c2], axis=-1),
            jnp.concatenate([-s1, s1, -s2, s2], axis=-1))


def _tiled_spec(tm):
    return pl.BlockSpec((tm * SUBLANES, LANES), lambda i: (i, 0))


def _rows_spec(x, tm):
    return _tiled_spec(tm) if x.shape[1] == LANES else pl.BlockSpec((tm, x.shape[1]), lambda i: (i, 0))


def _num_tokens(x):
    return x.shape[0] // SUBLANES if x.shape[1] == LANES else x.shape[0]


def _even_in_kernel(x_ref, g_ref, w_ref, cos_ref, sin_ref,
                    q_ref, ka_ref, kb_ref, va_ref, vb_ref, u_ref):
    h = _rms(_load_rows(x_ref), g_ref[...]).astype(bf16)
    proj = jnp.dot(h, w_ref[...], preferred_element_type=f32)
    cos = cos_ref[...]
    sin = sin_ref[...]
    scale = HEAD_DIM_A ** -0.5 * LOG2_E
    for c in range(A_Q // LANES):
        qc = proj[:, c * LANES:(c + 1) * LANES]
        q_ref[:, c * LANES:(c + 1) * LANES] = (_rope(qc, cos, sin) * scale).astype(bf16)
    kc = _rope(proj[:, A_Q:A_Q + A_KV], cos, sin)
    ka_ref[...] = kc.astype(bf16)
    kb_ref[...] = pltpu.roll(kc, HEAD_DIM_A, 1).astype(bf16)
    vc = proj[:, A_Q + A_KV:A_Q + 2 * A_KV]
    va_ref[...] = vc.astype(bf16)
    vb_ref[...] = pltpu.roll(vc, HEAD_DIM_A, 1).astype(bf16)
    u_ref[...] = proj[:, A_Q + 2 * A_KV:]


def _even_in(xt, g, w_bf, cos, sin, seq):
    n = _num_tokens(xt)
    d = SUBLANES * LANES
    tm = ROW_TILE
    per_seq = seq // tm
    row = lambda i: (i, 0)
    tab = lambda i: (i % per_seq, 0)
    full = lambda i: (0, 0)
    return pl.pallas_call(
        _even_in_kernel,
        grid=(n // tm,),
        in_specs=[_rows_spec(xt, tm), pl.BlockSpec((1, d), full),
                  pl.BlockSpec(w_bf.shape, full),
                  pl.BlockSpec((tm, LANES), tab), pl.BlockSpec((tm, LANES), tab)],
        out_specs=[pl.BlockSpec((tm, A_Q), row)] + [pl.BlockSpec((tm, A_KV), row)] * 4
        + [pl.BlockSpec((tm, POOL_WIDTH), row)],
        out_shape=[jax.ShapeDtypeStruct((n, A_Q), bf16)] + [jax.ShapeDtypeStruct((n, A_KV), bf16)] * 4
        + [jax.ShapeDtypeStruct((n, POOL_WIDTH), f32)],
        compiler_params=pltpu.CompilerParams(dimension_semantics=("arbitrary",)),
        name="even_in_proj",
    )(xt, g, w_bf, cos, sin)


def _win_attn_kernel(sink_ref, q_ref, kap, kam, kan, kbp, kbm, kbn, vap, vam, van, vbp, vbm, vbn,
                     o_ref, *, seq):
    i = pl.program_id(1)
    tq = q_ref.shape[0]
    nk = tq + 2 * WINDOW
    cat = lambda p, m, n: jnp.concatenate([p[...], m[...], n[...]], axis=0)
    ka, kb, va, vb = cat(kap, kam, kan), cat(kbp, kbm, kbn), cat(vap, vam, van), cat(vbp, vbm, vbn)
    lo_k = lax.broadcasted_iota(i32, (nk, LANES), 1) < HEAD_DIM_A
    lo_q = lax.broadcasted_iota(i32, (tq, LANES), 1) < HEAD_DIM_A
    zero = jnp.zeros((nk, LANES), bf16)
    ones = jnp.ones((nk, LANES), bf16)
    qpos = i * tq + lax.broadcasted_iota(i32, (tq, nk), 0)
    kpos = i * tq - WINDOW + lax.broadcasted_iota(i32, (tq, nk), 1)
    valid = (jnp.abs(qpos - kpos) <= WINDOW) & (kpos >= 0) & (kpos < seq)
    bias = jnp.where(valid, 0.0, NEG_INF)
    k_lo = {0: jnp.where(lo_k, ka, zero), 1: jnp.where(lo_k, kb, zero)}
    k_hi = {0: jnp.where(lo_k, zero, kb), 1: jnp.where(lo_k, zero, ka)}
    v_lo = {0: jnp.concatenate([va, ones], axis=1), 1: jnp.concatenate([vb, ones], axis=1)}
    v_hi = {0: v_lo[1], 1: v_lo[0]}
    for c in range(A_Q // LANES):
        kv = c // (A_Q // LANES // N_KV_A)
        qc = q_ref[:, c * LANES:(c + 1) * LANES]
        halves = []
        for half in range(2):
            kz, vz = (k_lo[kv], v_lo[kv]) if half == 0 else (k_hi[kv], v_hi[kv])
            s = lax.dot_general(qc, kz, (((1,), (1,)), ((), ())), preferred_element_type=f32) + bias
            sk = sink_ref[2 * c + half] * LOG2_E
            m = jnp.maximum(jnp.max(s, axis=1, keepdims=True), sk)
            pv = jnp.dot(jnp.exp2(s - m).astype(bf16), vz, preferred_element_type=f32)
            halves.append(pv[:, :LANES] / (pv[:, LANES:] + jnp.exp2(sk - m)))
        o_ref[:, c * LANES:(c + 1) * LANES] = jnp.where(lo_q, halves[0], halves[1]).astype(bf16)


def _win_attn(sink, q3, ka3, kb3, va3, vb3):
    b, seq, _ = q3.shape
    tq = WIN_TQ
    r = tq // WINDOW
    nwb = seq // WINDOW
    main = lambda bi, i: (bi, i, 0)
    prev = lambda bi, i: (bi, jnp.maximum(i * r - 1, 0), 0)
    nxt = lambda bi, i: (bi, jnp.minimum((i + 1) * r, nwb - 1), 0)
    kv_specs = [pl.BlockSpec((None, WINDOW, A_KV), prev), pl.BlockSpec((None, tq, A_KV), main),
                pl.BlockSpec((None, WINDOW, A_KV), nxt)]
    return pl.pallas_call(
        functools.partial(_win_attn_kernel, seq=seq),
        grid=(b, seq // tq),
        in_specs=[pl.BlockSpec(memory_space=pltpu.SMEM), pl.BlockSpec((None, tq, A_Q), main)] + kv_specs * 4,
        out_specs=pl.BlockSpec((None, tq, A_Q), main),
        out_shape=jax.ShapeDtypeStruct((b, seq, A_Q), bf16),
        compiler_params=pltpu.CompilerParams(dimension_semantics=("arbitrary", "arbitrary")),
        name="window_attention",
    )(sink, q3, ka3, ka3, ka3, kb3, kb3, kb3, va3, va3, va3, vb3, vb3, vb3)


def _post_mixer(xn, r0, g_ref, wr_ref, xo_ref, h_ref, aff_ref):
    _store_rows(xo_ref, xn, r0)
    h2 = _rms(xn, g_ref[...])
    _store_rows(h_ref, h2, r0)
    wr = wr_ref[...]
    h_hi, w_hi = h2.astype(bf16), wr.astype(bf16)
    h_lo, w_lo = (h2 - h_hi.astype(f32)).astype(bf16), (wr - w_hi.astype(f32)).astype(bf16)
    logits = (jnp.dot(h_hi, w_hi, preferred_element_type=f32) + jnp.dot(h_lo, w_hi, preferred_element_type=f32)
              + jnp.dot(h_hi, w_lo, preferred_element_type=f32))
    lane = lax.broadcasted_iota(i32, logits.shape, 1)
    logits = jnp.where(lane < N_EXPERTS, logits, NEG_INF)
    e = jnp.exp(logits - jnp.max(logits, axis=1, keepdims=True))
    aff = e / jnp.sum(e, axis=1, keepdims=True)
    aff_ref[:, r0:r0 + xn.shape[0]] = aff.T[:N_EXPERTS, :]


def _post_specs(n, seq, tm, b):
    per_seq = seq // tm
    out_specs = [_tiled_spec(tm), _tiled_spec(tm),
                 pl.BlockSpec((None, N_EXPERTS, tm), lambda i: (i // per_seq, 0, i % per_seq))]
    out_shape = [jax.ShapeDtypeStruct((n * SUBLANES, LANES), f32), jax.ShapeDtypeStruct((n * SUBLANES, LANES), f32),
                 jax.ShapeDtypeStruct((b, N_EXPERTS, seq), f32)]
    return out_specs, out_shape


def _even_out_kernel(attn_ref, um_ref, up_ref, un_ref, x_ref, wo_ref, wp_ref, sc_ref, g_ref, wr_ref,
                     xo_ref, h_ref, aff_ref, ubuf, *, seq):
    tm = attn_ref.shape[0]
    halo = SUBLANES
    pos0 = (pl.program_id(0) % (seq // tm)) * tm
    ubuf[0:halo, :] = jnp.where(pos0 > 0, up_ref[...], 0.0)
    ubuf[halo:halo + tm, :] = um_ref[...]
    ubuf[halo + tm:, :] = jnp.where(pos0 + tm < seq, un_ref[...], 0.0)
    tu = tm // ROW_UNITS
    for r0 in range(0, tm, tu):
        t = pos0 + r0 + lax.broadcasted_iota(i32, (tu, 1), 0)
        mixer = [attn_ref[r0:r0 + tu, :]]
        for g, w in enumerate(POOL_WINDOWS):
            hw = w // 2
            cols = slice(g * POOL_GROUP, (g + 1) * POOL_GROUP)
            base = halo + r0
            acc = ubuf[base - hw:base - hw + tu, cols]
            for o in range(-hw + 1, hw):
                acc = acc + ubuf[base + o:base + o + tu, cols]
            cnt = (jnp.minimum(t + hw, seq) - jnp.maximum(t - hw, 0)).astype(f32)
            mixed = acc / cnt - ubuf[base:base + tu, cols]
            y = jnp.dot(mixed.astype(bf16), wp_ref[g], preferred_element_type=f32) * sc_ref[:, cols]
            mixer.append(y.astype(bf16))
        proj = jnp.dot(jnp.concatenate(mixer, axis=1), wo_ref[...], preferred_element_type=f32)
        _post_mixer(_load_rows(x_ref, r0, tu) + proj, r0, g_ref, wr_ref, xo_ref, h_ref, aff_ref)


def _even_out(attn2, u2, xt, wo_bf, wp_bf, scale, g, wr, seq, b):
    n = _num_tokens(xt)
    d = SUBLANES * LANES
    tm = ROW_TILE
    halo = SUBLANES
    row = lambda i: (i, 0)
    full2 = lambda i: (0, 0)
    nhb = n // halo
    out_specs, out_shape = _post_specs(n, seq, tm, b)
    return pl.pallas_call(
        functools.partial(_even_out_kernel, seq=seq),
        grid=(n // tm,),
        in_specs=[pl.BlockSpec((tm, A_Q), row), pl.BlockSpec((tm, POOL_WIDTH), row),
                  pl.BlockSpec((halo, POOL_WIDTH), lambda i: (jnp.maximum(i * (tm // halo) - 1, 0), 0)),
                  pl.BlockSpec((halo, POOL_WIDTH), lambda i: (jnp.minimum((i + 1) * (tm // halo), nhb - 1), 0)),
                  _rows_spec(xt, tm), pl.BlockSpec(wo_bf.shape, full2),
                  pl.BlockSpec(wp_bf.shape, lambda i: (0, 0, 0)), pl.BlockSpec((1, POOL_WIDTH), full2),
                  pl.BlockSpec((1, d), full2), pl.BlockSpec(wr.shape, full2)],
        out_specs=out_specs, out_shape=out_shape,
        scratch_shapes=[pltpu.VMEM((tm + 2 * halo, POOL_WIDTH), f32)],
        compiler_params=pltpu.CompilerParams(dimension_semantics=("arbitrary",)),
        name="even_out_proj",
    )(attn2, u2, u2, u2, xt, wo_bf, wp_bf, scale, g, wr)


def _odd_in_kernel(x_ref, g_ref, w_ref, qg_ref, kg_ref, cos_ref, sin_ref, q_ref, k_ref, v_ref):
    scale = HEAD_DIM_C ** -0.5 * LOG2_E
    tm = q_ref.shape[0]
    tu = tm // ROW_UNITS
    for r0 in range(0, tm, tu):
        rows = slice(r0, r0 + tu)
        h = _rms(_load_rows(x_ref, r0, tu), g_ref[...]).astype(bf16)
        proj = jnp.dot(h, w_ref[...], preferred_element_type=f32)
        cos = cos_ref[rows, :]
        sin = sin_ref[rows, :]
        for c in range(N_HEADS_C):
            qc = _rms(proj[:, c * LANES:(c + 1) * LANES], qg_ref[...])
            q_ref[rows, c * LANES:(c + 1) * LANES] = (_rope(qc, cos, sin) * scale).astype(bf16)
        for c in range(N_KV_C):
            kc = _rms(proj[:, C_Q + c * LANES:C_Q + (c + 1) * LANES], kg_ref[...])
            k_ref[rows, c * LANES:(c + 1) * LANES] = _rope(kc, cos, sin).astype(bf16)
            vc = proj[:, C_Q + C_KV + c * LANES:C_Q + C_KV + (c + 1) * LANES]
            v_ref[rows, 2 * c * LANES:(2 * c + 1) * LANES] = vc.astype(bf16)
            v_ref[rows, (2 * c + 1) * LANES:(2 * c + 2) * LANES] = jnp.ones((tu, LANES), bf16)


def _odd_in(xt, g, w_bf, qg, kg, cos, sin, seq):
    n = _num_tokens(xt)
    d = SUBLANES * LANES
    tm = ROW_TILE
    per_seq = seq // tm
    row = lambda i: (i, 0)
    tab = lambda i: (i % per_seq, 0)
    full = lambda i: (0, 0)
    return pl.pallas_call(
        _odd_in_kernel,
        grid=(n // tm,),
        in_specs=[_tiled_spec(tm), pl.BlockSpec((1, d), full), pl.BlockSpec(w_bf.shape, full),
                  pl.BlockSpec((1, HEAD_DIM_C), full), pl.BlockSpec((1, HEAD_DIM_C), full),
                  pl.BlockSpec((tm, LANES), tab), pl.BlockSpec((tm, LANES), tab)],
        out_specs=[pl.BlockSpec((tm, C_Q), row), pl.BlockSpec((tm, C_KV), row), pl.BlockSpec((tm, 2 * C_KV), row)],
        out_shape=[jax.ShapeDtypeStruct((n, C_Q), bf16), jax.ShapeDtypeStruct((n, C_KV), bf16),
                   jax.ShapeDtypeStruct((n, 2 * C_KV), bf16)],
        compiler_params=pltpu.CompilerParams(dimension_semantics=("arbitrary",)),
        name="odd_in_proj",
    )(xt, g, w_bf, qg, kg, cos, sin)


def _flash_kernel(q_ref, k_ref, v_ref, o_ref, m_sc, acc_sc, *, tk):
    tq = q_ref.shape[0]
    rows_per_unit = tq // FLASH_ROW_SPLIT
    m_sc[...] = jnp.full(m_sc.shape, NEG_INF, f32)
    acc_sc[...] = jnp.zeros(acc_sc.shape, f32)

    def key_tile(ki, carry):
        k0 = pl.multiple_of(ki * tk, tk)
        k = k_ref[pl.ds(k0, tk), :]
        v = v_ref[pl.ds(k0, tk), :]
        for g in range(GROUP_C):
            for r0 in range(0, tq, rows_per_unit):
                rows = slice(r0, r0 + rows_per_unit)
                qg = q_ref[rows, g * LANES:(g + 1) * LANES]
                s = lax.dot_general(qg, k, (((1,), (1,)), ((), ())), preferred_element_type=f32)
                m_prev = m_sc[g, rows]
                m_new = jnp.maximum(m_prev, jnp.max(s, axis=1, keepdims=True))
                p = jnp.exp2(s - m_new).astype(bf16)
                acc_sc[g, rows] = (jnp.exp2(m_prev - m_new) * acc_sc[g, rows]
                                   + jnp.dot(p, v, preferred_element_type=f32))
                m_sc[g, rows] = m_new
        return carry

    lax.fori_loop(0, k_ref.shape[0] // tk, key_tile, 0)
    for g in range(GROUP_C):
        acc = acc_sc[g]
        o_ref[:, g * LANES:(g + 1) * LANES] = (acc[:, :HEAD_DIM_C] / acc[:, HEAD_DIM_C:]).astype(bf16)


def _flash(q3, k3, v3):
    b, seq, _ = q3.shape
    tq, tk = FLASH_TQ, min(FLASH_TK, seq)
    gw = GROUP_C * HEAD_DIM_C
    return pl.pallas_call(
        functools.partial(_flash_kernel, tk=tk),
        grid=(b, N_KV_C, seq // tq),
        in_specs=[pl.BlockSpec((None, tq, gw), lambda bi, j, qi: (bi, qi, j)),
                  pl.BlockSpec((None, seq, HEAD_DIM_C), lambda bi, j, qi: (bi, 0, j)),
                  pl.BlockSpec((None, seq, 2 * HEAD_DIM_C), lambda bi, j, qi: (bi, 0, j))],
        out_specs=pl.BlockSpec((None, tq, gw), lambda bi, j, qi: (bi, qi, j)),
        out_shape=jax.ShapeDtypeStruct((b, seq, C_Q), bf16),
        scratch_shapes=[pltpu.VMEM((GROUP_C, tq, 1), f32), pltpu.VMEM((GROUP_C, tq, 2 * HEAD_DIM_C), f32)],
        compiler_params=pltpu.CompilerParams(dimension_semantics=("arbitrary", "arbitrary", "arbitrary")),
        name="dense_attention",
    )(q3, k3, v3)


def _odd_out_kernel(attn_ref, x_ref, wo_ref, g_ref, wr_ref, xo_ref, h_ref, aff_ref):
    tm = attn_ref.shape[0]
    tu = tm // ROW_UNITS
    for r0 in range(0, tm, tu):
        proj = jnp.dot(attn_ref[r0:r0 + tu, :], wo_ref[...], preferred_element_type=f32)
        _post_mixer(_load_rows(x_ref, r0, tu) + proj, r0, g_ref, wr_ref, xo_ref, h_ref, aff_ref)


def _odd_out(attn2, xt, wo_bf, g, wr, seq, b):
    n = _num_tokens(xt)
    d = SUBLANES * LANES
    tm = ROW_TILE
    row = lambda i: (i, 0)
    full2 = lambda i: (0, 0)
    out_specs, out_shape = _post_specs(n, seq, tm, b)
    return pl.pallas_call(
        _odd_out_kernel,
        grid=(n // tm,),
        in_specs=[pl.BlockSpec((tm, C_Q), row), _tiled_spec(tm), pl.BlockSpec(wo_bf.shape, full2),
                  pl.BlockSpec((1, d), full2), pl.BlockSpec(wr.shape, full2)],
        out_specs=out_specs, out_shape=out_shape,
        compiler_params=pltpu.CompilerParams(dimension_semantics=("arbitrary",)),
        name="odd_out_proj",
    )(attn2, xt, wo_bf, g, wr)


def _topk_kernel(aff_ref, idx_ref, *, cap):
    ne, r, _ = aff_ref.shape
    er = ne * r
    shift = r.bit_length() - 1
    aff = aff_ref[...]

    def count(mask):
        return jnp.sum(jnp.sum(mask.astype(f32), axis=2, keepdims=True), axis=1, keepdims=True)

    def bit_step(it, bits):
        cand = bits | jnp.left_shift(jnp.int32(1), 30 - it)
        return jnp.where(count(aff >= lax.bitcast_convert_type(cand, f32)) >= cap, cand, bits)

    bits = lax.fori_loop(0, 31, bit_step, jnp.zeros((ne, 1, 1), i32))
    gt = aff >= lax.bitcast_convert_type(bits + 1, f32)
    eq = (aff >= lax.bitcast_convert_type(bits, f32)) & jnp.logical_not(gt)
    need = cap - count(gt)

    li = lax.broadcasted_iota(i32, (LANES, LANES), 0)
    lj = lax.broadcasted_iota(i32, (LANES, LANES), 1)
    ri = lax.broadcasted_iota(i32, (er, er), 0)
    rj = lax.broadcasted_iota(i32, (er, er), 1)
    rows_before = (((ri >> shift) == (rj >> shift)) & (rj < ri)).astype(bf16)
    ones_sq = jnp.ones((LANES, LANES), bf16)

    def row_stats(mask3, inclusive):
        x = mask3.astype(bf16).reshape(er, LANES)
        tri = ((li <= lj) if inclusive else (li < lj)).astype(bf16)
        within = jnp.dot(x, tri, preferred_element_type=f32)
        tot = jnp.dot(x, ones_sq, preferred_element_type=f32)
        before = jnp.dot(rows_before, tot.astype(bf16), preferred_element_type=f32)
        return within, tot, before

    w_eq, _, b_eq = row_stats(eq, False)
    sel = gt | (eq & ((w_eq + b_eq).reshape(ne, r, LANES) < need))
    within, tot, before = row_stats(sel, True)

    lane_f = lax.broadcasted_iota(i32, (r, LANES), 1).astype(f32)
    pad = LANES - r
    for e in range(ne):
        rows = slice(e * r, (e + 1) * r)
        pre, tt, we = before[rows], tot[rows], within[rows]
        inc = pre + tt
        if pad:
            we = jnp.concatenate([we, jnp.zeros((pad, LANES), f32)], axis=0)
        we_t = we.T.astype(bf16)
        row_of, rank, hot = [], [], []
        for c in range(cap // LANES):
            s = lane_f + float(c * LANES)
            done = inc <= s
            row_of.append(jnp.sum(done.astype(f32), axis=0, keepdims=True))
            rank.append(s[0:1, :] - jnp.sum(jnp.where(done, tt, 0.0), axis=0, keepdims=True))
            hot.append(((pre <= s) & jnp.logical_not(done)).astype(bf16))
        hot = jnp.concatenate(hot, axis=1)
        if pad:
            hot = jnp.concatenate([hot, jnp.zeros((pad, cap), bf16)], axis=0)
        run = jnp.dot(we_t, hot, preferred_element_type=f32)
        lane_of = jnp.sum((run <= jnp.concatenate(rank, axis=1)).astype(f32), axis=0, keepdims=True)
        idx_ref[e] = (jnp.concatenate(row_of, axis=1) * LANES + lane_of).astype(i32)


def _topk(aff4, cap):
    b, ne, r, _ = aff4.shape
    assert r & (r - 1) == 0 and r <= LANES and r % SUBLANES == 0 and cap % LANES == 0
    return pl.pallas_call(
        functools.partial(_topk_kernel, cap=cap),
        grid=(b,),
        in_specs=[pl.BlockSpec((None, ne, r, LANES), lambda bi: (bi, 0, 0, 0))],
        out_specs=pl.BlockSpec((None, ne, 1, cap), lambda bi: (bi, 0, 0, 0)),
        out_shape=jax.ShapeDtypeStruct((b, ne, 1, cap), i32),
        compiler_params=pltpu.CompilerParams(dimension_semantics=("arbitrary",)),
        name="expert_choice_topk",
    )(aff4)


def _ffn_kernel(idx_ref, idxn_ref, idxp_ref, affp_ref, h_ref, x_ref, wg_ref, wu_ref, wd_ref, out_ref,
                acc, xsbuf, xs, yacc, ydone, gsem, xsem, *, seq, cap, ne):
    n, j = pl.program_id(0), pl.program_id(1)
    n_exp, nj = pl.num_programs(0) - 1, pl.num_programs(1)
    tile = SUBLANES
    rows_per_step = cap // nj
    batch_cur = jnp.minimum(n, n_exp - 1) // ne
    batch_next = jnp.minimum(n + 1, n_exp - 1) // ne
    batch_prev = jnp.maximum(n - 1, 0) // ne

    def token(ref, t):
        return ref.at[pl.ds(pl.multiple_of(t * tile, tile), tile), :]

    def gather_start(token_row, s):
        pltpu.make_async_copy(token(h_ref, token_row), token(xsbuf, s), gsem.at[0]).start()

    def gather_wait():
        pltpu.make_async_copy(h_ref.at[pl.ds(0, cap * tile), :], xsbuf, gsem.at[0]).wait()

    def acc_load(batch):
        cp = pltpu.make_async_copy(x_ref.at[pl.ds(batch * seq * tile, seq * tile), :], acc, xsem.at[0])
        cp.start()
        cp.wait()

    def scatter_prev_rows():
        for g0 in range(0, rows_per_step, SCATTER_GROUP):
            ss = [j * rows_per_step + g0 + k for k in range(SCATTER_GROUP)]
            ts = [idxp_ref[0, s] for s in ss]
            new = [token(acc, t)[...] + affp_ref[0, t] * token(ydone, s)[...] for s, t in zip(ss, ts)]
            for t, v in zip(ts, new):
                token(acc, t)[...] = v

    @pl.when((n == 0) & (j == 0))
    def _():
        def trip(i, c):
            for k in range(ROW_UNROLL):
                s = i * ROW_UNROLL + k
                gather_start(idx_ref[0, s], s)
            return c
        lax.fori_loop(0, cap // ROW_UNROLL, trip, 0)
        ydone[...] = jnp.zeros(ydone.shape, f32)
        acc_load(0)

    @pl.when((j == 0) & (n < n_exp))
    def _():
        gather_wait()
        xs[...] = _load_rows(xsbuf).astype(bf16)
        yacc[...] = jnp.zeros(yacc.shape, f32)

    @pl.when(n < n_exp)
    def _():
        for k in range(rows_per_step):
            s = j * rows_per_step + k
            gather_start(batch_next * seq + idxn_ref[0, s], s)
        xv = xs[...]
        gate = jnp.dot(xv, wg_ref[...].astype(bf16), preferred_element_type=f32)
        up = jnp.dot(xv, wu_ref[...].astype(bf16), preferred_element_type=f32)
        hm = (gate * jax.nn.sigmoid(gate) * up).astype(bf16)
        yacc[...] += jnp.dot(hm, wd_ref[...].astype(bf16), preferred_element_type=f32)
        scatter_prev_rows()

    @pl.when(n == n_exp)
    def _():
        scatter_prev_rows()

    @pl.when((j == nj - 1) & (n < n_exp))
    def _():
        _store_rows(ydone, yacc[...])

    @pl.when((j == nj - 1) & (n == n_exp - 1))
    def _():
        gather_wait()

    @pl.when((j == nj - 1) & (n > 0) & (n % ne == 0))
    def _():
        cp = pltpu.make_async_copy(acc, out_ref.at[pl.ds(batch_prev * seq * tile, seq * tile), :], xsem.at[1])
        cp.start()
        cp.wait()

        @pl.when(n < n_exp)
        def _():
            acc_load(batch_cur)


def _ffn(idx3, aff3, ht, xt, wg, wu, wd, layer, b, seq, cap):
    n = _num_tokens(xt)
    d = SUBLANES * LANES
    ne = N_EXPERTS
    n_exp = b * ne
    tf = FF_TILE
    nj = wg.shape[-1] // tf
    cur = lambda i, j: (jnp.minimum(i, n_exp - 1), 0, 0)
    nxt = lambda i, j: (jnp.minimum(i + 1, n_exp - 1), 0, 0)
    prv = lambda i, j: (jnp.maximum(i - 1, 0), 0, 0)
    expert = lambda i: jnp.minimum(i, n_exp - 1) % ne
    col = lambda i, j: jnp.where(i < n_exp, j, nj - 1)
    smem = pltpu.SMEM
    return pl.pallas_call(
        functools.partial(_ffn_kernel, seq=seq, cap=cap, ne=ne),
        grid=(n_exp + 1, nj),
        in_specs=[pl.BlockSpec((None, 1, cap), cur, memory_space=smem),
                  pl.BlockSpec((None, 1, cap), nxt, memory_space=smem),
                  pl.BlockSpec((None, 1, cap), prv, memory_space=smem),
                  pl.BlockSpec((None, 1, seq), prv, memory_space=smem),
                  pl.BlockSpec(memory_space=pl.ANY), pl.BlockSpec(memory_space=pl.ANY),
                  pl.BlockSpec((None, None, d, tf), lambda i, j: (layer, expert(i), 0, col(i, j))),
                  pl.BlockSpec((None, None, d, tf), lambda i, j: (layer, expert(i), 0, col(i, j))),
                  pl.BlockSpec((None, None, tf, d), lambda i, j: (layer, expert(i), col(i, j), 0))],
        out_specs=pl.BlockSpec(memory_space=pl.ANY),
        out_shape=jax.ShapeDtypeStruct((n * SUBLANES, LANES), f32),
        scratch_shapes=[pltpu.VMEM((seq * SUBLANES, LANES), f32), pltpu.VMEM((cap * SUBLANES, LANES), f32),
                        pltpu.VMEM((cap, d), bf16), pltpu.VMEM((cap, d), f32),
                        pltpu.VMEM((cap * SUBLANES, LANES), f32),
                        pltpu.SemaphoreType.DMA((1,)), pltpu.SemaphoreType.DMA((2,))],
        compiler_params=pltpu.CompilerParams(
            dimension_semantics=("arbitrary", "arbitrary"),
            vmem_limit_bytes=VMEM_LIMIT_FFN),
        name="expert_ffn",
    )(idx3, idx3, idx3, aff3, ht, xt, wg, wu, wd)


def _final_norm_kernel(x_ref, g_ref, o_ref):
    o_ref[...] = _rms(_load_rows(x_ref), g_ref[...])


def _final_norm(xt, g):
    n = _num_tokens(xt)
    d = SUBLANES * LANES
    tm = ROW_TILE
    return pl.pallas_call(
        _final_norm_kernel,
        grid=(n // tm,),
        in_specs=[_tiled_spec(tm), pl.BlockSpec((1, d), lambda i: (0, 0))],
        out_specs=pl.BlockSpec((tm, d), lambda i: (i, 0)),
        out_shape=jax.ShapeDtypeStruct((n, d), f32),
        compiler_params=pltpu.CompilerParams(dimension_semantics=("arbitrary",)),
        name="final_norm",
    )(xt, g)


def kernel(x, norm_mix, norm_ffn, norm_final, a_w_in, a_w_out, a_sink, b_w_pool, b_scale,
           c_w_qkv, c_q_norm, c_k_norm, c_w_out, moe_router, moe_w_gate, moe_w_up, moe_w_down):
    b, seq, d = x.shape
    depth = norm_mix.shape[0]
    n = b * seq
    cap = CAPACITY_FACTOR * seq // N_EXPERTS
    assert d == SUBLANES * LANES
    assert seq % ROW_TILE == 0 and seq % FLASH_TQ == 0 and seq % min(FLASH_TK, seq) == 0 and seq % WIN_TQ == 0
    rows_per_step = cap // (moe_w_gate.shape[-1] // FF_TILE)
    assert rows_per_step % SCATTER_GROUP == 0 and cap % ROW_UNROLL == 0

    t = jnp.arange(seq, dtype=i32)
    cos_a, sin_a = _rope_tables(t, t, HEAD_DIM_A)
    cos_c, sin_c = _rope_tables(t // GRID_W, t % GRID_W, AXIAL_DIM)
    wr_pad = jnp.pad(moe_router, ((0, 0), (0, 0), (0, LANES - N_EXPERTS)))

    xt = x.reshape(n, d)
    for i in range(depth):
        j = i // 2
        g_mix = norm_mix[i].reshape(1, d)
        g_ffn = norm_ffn[i].reshape(1, d)
        if i % 2 == 0:
            q, ka, kb, va, vb, u = _even_in(xt, g_mix, a_w_in[j].astype(bf16), cos_a, sin_a, seq)
            r3 = lambda a: a.reshape(b, seq, a.shape[-1])
            attn = _win_attn(a_sink[j], r3(q), r3(ka), r3(kb), r3(va), r3(vb))
            xt, ht, aff = _even_out(attn.reshape(n, A_Q), u, xt, a_w_out[j].astype(bf16),
                                    b_w_pool[j].astype(bf16), b_scale[j].reshape(1, POOL_WIDTH),
                                    g_ffn, wr_pad[i], seq, b)
        else:
            q, k, v = _odd_in(xt, g_mix, c_w_qkv[j].astype(bf16), c_q_norm[j].reshape(1, HEAD_DIM_C),
                              c_k_norm[j].reshape(1, HEAD_DIM_C), cos_c, sin_c, seq)
            attn = _flash(q.reshape(b, seq, C_Q), k.reshape(b, seq, C_KV), v.reshape(b, seq, 2 * C_KV))
            xt, ht, aff = _odd_out(attn.reshape(n, C_Q), xt, c_w_out[j].astype(bf16), g_ffn, wr_pad[i], seq, b)
        idx = _topk(aff.reshape(b, N_EXPERTS, seq // LANES, LANES), cap)
        xt = _ffn(idx.reshape(b * N_EXPERTS, 1, cap), aff.reshape(b * N_EXPERTS, 1, seq), ht, xt,
                  moe_w_gate, moe_w_up, moe_w_down, i, b, seq, cap)
    return _final_norm(xt, norm_final.reshape(1, d)).reshape(b, seq, d)
```

```python
import functools

import jax
import jax.numpy as jnp
import numpy as np
from jax import lax
from jax.experimental import pallas as pl
from jax.experimental.pallas import tpu as pltpu

f32 = jnp.float32
bf16 = jnp.bfloat16
i32 = jnp.int32

ROPE_THETA = 10000.0
RMS_EPS = 1e-6
NEG_INF = -1e30
GRID_W = 64
N_HEADS_A, N_KV_A, HEAD_DIM_A = 8, 2, 64
WINDOW = 128
A_Q = N_HEADS_A * HEAD_DIM_A
A_KV = N_KV_A * HEAD_DIM_A
POOL_WINDOWS = (2, 4, 8, 16)
POOL_GROUP = 128
POOL_WIDTH = POOL_GROUP * len(POOL_WINDOWS)
N_HEADS_C, N_KV_C, HEAD_DIM_C = 8, 2, 128
GROUP_C = N_HEADS_C // N_KV_C
C_Q = N_HEADS_C * HEAD_DIM_C
C_KV = N_KV_C * HEAD_DIM_C
AXIAL_DIM = HEAD_DIM_C // 2
N_EXPERTS = 16
CAPACITY_FACTOR = 2

LANES = 128
SUBLANES = 8
ROW_TILE = 512
ROW_UNITS = 2
WIN_TQ = 256
FLASH_TQ = 1024
FLASH_TK = 4096
FLASH_ROW_SPLIT = 4
LOG2_E = 1.4426950408889634
FF_TILE = 256
ROW_UNROLL = 8
SCATTER_GROUP = 4
VMEM_LIMIT_FFN = 60 * 1024 * 1024


def _rms(x, g):
    ms = jnp.mean(x * x, axis=-1, keepdims=True)
    return x * lax.rsqrt(ms + RMS_EPS) * g


def _load_rows(ref, r0=0, rows=None):
    if ref.shape[1] != LANES:
        return ref[...] if rows is None else ref[r0:r0 + rows, :]
    rows = ref.shape[0] // SUBLANES if rows is None else rows
    return jnp.concatenate([ref[pl.ds(r0 * SUBLANES + c, rows, stride=SUBLANES), :] for c in range(SUBLANES)],
                           axis=1)


def _store_rows(ref, val, r0=0):
    for c in range(SUBLANES):
        ref[pl.ds(r0 * SUBLANES + c, val.shape[0], stride=SUBLANES), :] = val[:, c * LANES:(c + 1) * LANES]


def _rot_half32(x):
    lane = lax.broadcasted_iota(i32, x.shape, 1)
    first = (lane & 63) < 32
    return jnp.where(first, pltpu.roll(x, LANES - 32, 1), pltpu.roll(x, 32, 1))


def _rope(x, cos, sin_signed):
    return x * cos + _rot_half32(x) * sin_signed


def _rope_tables(pos1, pos2, dim):
    inv = ROPE_THETA ** (-np.arange(0, dim, 2, dtype=np.float64) / dim)

    def cs(pos):
        ang = pos.astype(np.float64)[:, None] * inv[None, :]
        return np.cos(ang), np.sin(ang)

    c1, s1 = cs(pos1)
    c2, s2 = cs(pos2)
    return (jnp.asarray(np.concatenate([c1, c1, c2, c2], axis=-1), f32),
            jnp.asarray(np.concatenate([-s1, s1, -s2, s2], axis=-1), f32))


def _tiled_spec(tm):
    return pl.BlockSpec((tm * SUBLANES, LANES), lambda i: (i, 0))


def _rows_spec(x, tm):
    return _tiled_spec(tm) if x.shape[1] == LANES else pl.BlockSpec((tm, x.shape[1]), lambda i: (i, 0))


def _num_tokens(x):
    return x.shape[0] // SUBLANES if x.shape[1] == LANES else x.shape[0]


def _even_in_kernel(x_ref, g_ref, w_ref, cos_ref, sin_ref,
                    q_ref, ka_ref, kb_ref, va_ref, vb_ref, u_ref):
    h = _rms(_load_rows(x_ref), g_ref[...]).astype(bf16)
    proj = jnp.dot(h, w_ref[...], preferred_element_type=f32)
    cos = cos_ref[...]
    sin = sin_ref[...]
    scale = HEAD_DIM_A ** -0.5 * LOG2_E
    for c in range(A_Q // LANES):
        qc = proj[:, c * LANES:(c + 1) * LANES]
        q_ref[:, c * LANES:(c + 1) * LANES] = (_rope(qc, cos, sin) * scale).astype(bf16)
    kc = _rope(proj[:, A_Q:A_Q + A_KV], cos, sin)
    ka_ref[...] = kc.astype(bf16)
    kb_ref[...] = pltpu.roll(kc, HEAD_DIM_A, 1).astype(bf16)
    vc = proj[:, A_Q + A_KV:A_Q + 2 * A_KV]
    va_ref[...] = vc.astype(bf16)
    vb_ref[...] = pltpu.roll(vc, HEAD_DIM_A, 1).astype(bf16)
    u_ref[...] = proj[:, A_Q + 2 * A_KV:]


def _even_in(xt, g, w_bf, cos, sin, seq):
    n = _num_tokens(xt)
    d = SUBLANES * LANES
    tm = ROW_TILE
    per_seq = seq // tm
    row = lambda i: (i, 0)
    tab = lambda i: (i % per_seq, 0)
    full = lambda i: (0, 0)
    return pl.pallas_call(
        _even_in_kernel,
        grid=(n // tm,),
        in_specs=[_rows_spec(xt, tm), pl.BlockSpec((1, d), full),
                  pl.BlockSpec(w_bf.shape, full),
                  pl.BlockSpec((tm, LANES), tab), pl.BlockSpec((tm, LANES), tab)],
        out_specs=[pl.BlockSpec((tm, A_Q), row)] + [pl.BlockSpec((tm, A_KV), row)] * 4
        + [pl.BlockSpec((tm, POOL_WIDTH), row)],
        out_shape=[jax.ShapeDtypeStruct((n, A_Q), bf16)] + [jax.ShapeDtypeStruct((n, A_KV), bf16)] * 4
        + [jax.ShapeDtypeStruct((n, POOL_WIDTH), f32)],
        compiler_params=pltpu.CompilerParams(dimension_semantics=("arbitrary",)),
        name="even_in_proj",
    )(xt, g, w_bf, cos, sin)


def _win_attn_kernel(sink_ref, q_ref, kap, kam, kan, kbp, kbm, kbn, vap, vam, van, vbp, vbm, vbn,
                     o_ref, *, seq):
    i = pl.program_id(1)
    tq = q_ref.shape[0]
    nk = tq + 2 * WINDOW
    cat = lambda p, m, n: jnp.concatenate([p[...], m[...], n[...]], axis=0)
    ka, kb, va, vb = cat(kap, kam, kan), cat(kbp, kbm, kbn), cat(vap, vam, van), cat(vbp, vbm, vbn)
    lo_k = lax.broadcasted_iota(i32, (nk, LANES), 1) < HEAD_DIM_A
    lo_q = lax.broadcasted_iota(i32, (tq, LANES), 1) < HEAD_DIM_A
    zero = jnp.zeros((nk, LANES), bf16)
    ones = jnp.ones((nk, LANES), bf16)
    qpos = i * tq + lax.broadcasted_iota(i32, (tq, nk), 0)
    kpos = i * tq - WINDOW + lax.broadcasted_iota(i32, (tq, nk), 1)
    valid = (jnp.abs(qpos - kpos) <= WINDOW) & (kpos >= 0) & (kpos < seq)
    bias = jnp.where(valid, 0.0, NEG_INF)
    k_lo = {0: jnp.where(lo_k, ka, zero), 1: jnp.where(lo_k, kb, zero)}
    k_hi = {0: jnp.where(lo_k, zero, kb), 1: jnp.where(lo_k, zero, ka)}
    v_lo = {0: jnp.concatenate([va, ones], axis=1), 1: jnp.concatenate([vb, ones], axis=1)}
    v_hi = {0: v_lo[1], 1: v_lo[0]}
    for c in range(A_Q // LANES):
        kv = c // (A_Q // LANES // N_KV_A)
        qc = q_ref[:, c * LANES:(c + 1) * LANES]
        halves = []
        for half in range(2):
            kz, vz = (k_lo[kv], v_lo[kv]) if half == 0 else (k_hi[kv], v_hi[kv])
            s = lax.dot_general(qc, kz, (((1,), (1,)), ((), ())), preferred_element_type=f32) + bias
            sk = sink_ref[2 * c + half] * LOG2_E
            m = jnp.maximum(jnp.max(s, axis=1, keepdims=True), sk)
            pv = jnp.dot(jnp.exp2(s - m).astype(bf16), vz, preferred_element_type=f32)
            halves.append(pv[:, :LANES] / (pv[:, LANES:] + jnp.exp2(sk - m)))
        o_ref[:, c * LANES:(c + 1) * LANES] = jnp.where(lo_q, halves[0], halves[1]).astype(bf16)


def _win_attn(sink, q3, ka3, kb3, va3, vb3):
    b, seq, _ = q3.shape
    tq = WIN_TQ
    r = tq // WINDOW
    nwb = seq // WINDOW
    main = lambda bi, i: (bi, i, 0)
    prev = lambda bi, i: (bi, jnp.maximum(i * r - 1, 0), 0)
    nxt = lambda bi, i: (bi, jnp.minimum((i + 1) * r, nwb - 1), 0)
    kv_specs = [pl.BlockSpec((None, WINDOW, A_KV), prev), pl.BlockSpec((None, tq, A_KV), main),
                pl.BlockSpec((None, WINDOW, A_KV), nxt)]
    return pl.pallas_call(
        functools.partial(_win_attn_kernel, seq=seq),
        grid=(b, seq // tq),
        in_specs=[pl.BlockSpec(memory_space=pltpu.SMEM), pl.BlockSpec((None, tq, A_Q), main)] + kv_specs * 4,
        out_specs=pl.BlockSpec((None, tq, A_Q), main),
        out_shape=jax.ShapeDtypeStruct((b, seq, A_Q), bf16),
        compiler_params=pltpu.CompilerParams(dimension_semantics=("arbitrary", "arbitrary")),
        name="window_attention",
    )(sink, q3, ka3, ka3, ka3, kb3, kb3, kb3, va3, va3, va3, vb3, vb3, vb3)


def _post_mixer(xn, r0, g_ref, wr_ref, xo_ref, h_ref, aff_ref):
    _store_rows(xo_ref, xn, r0)
    h2 = _rms(xn, g_ref[...])
    _store_rows(h_ref, h2, r0)
    wr = wr_ref[...]
    h_hi, w_hi = h2.astype(bf16), wr.astype(bf16)
    h_lo, w_lo = (h2 - h_hi.astype(f32)).astype(bf16), (wr - w_hi.astype(f32)).astype(bf16)
    logits = (jnp.dot(h_hi, w_hi, preferred_element_type=f32) + jnp.dot(h_lo, w_hi, preferred_element_type=f32)
              + jnp.dot(h_hi, w_lo, preferred_element_type=f32))
    lane = lax.broadcasted_iota(i32, logits.shape, 1)
    logits = jnp.where(lane < N_EXPERTS, logits, NEG_INF)
    e = jnp.exp(logits - jnp.max(logits, axis=1, keepdims=True))
    aff = e / jnp.sum(e, axis=1, keepdims=True)
    aff_ref[:, r0:r0 + xn.shape[0]] = aff.T[:N_EXPERTS, :]


def _post_specs(n, seq, tm, b):
    per_seq = seq // tm
    out_specs = [_tiled_spec(tm), _tiled_spec(tm),
                 pl.BlockSpec((None, N_EXPERTS, tm), lambda i: (i // per_seq, 0, i % per_seq))]
    out_shape = [jax.ShapeDtypeStruct((n * SUBLANES, LANES), f32), jax.ShapeDtypeStruct((n * SUBLANES, LANES), f32),
                 jax.ShapeDtypeStruct((b, N_EXPERTS, seq), f32)]
    return out_specs, out_shape


def _even_out_kernel(attn_ref, um_ref, up_ref, un_ref, x_ref, wo_ref, wp_ref, sc_ref, g_ref, wr_ref,
                     xo_ref, h_ref, aff_ref, ubuf, *, seq):
    tm = attn_ref.shape[0]
    halo = SUBLANES
    pos0 = (pl.program_id(0) % (seq // tm)) * tm
    ubuf[0:halo, :] = jnp.where(pos0 > 0, up_ref[...], 0.0)
    ubuf[halo:halo + tm, :] = um_ref[...]
    ubuf[halo + tm:, :] = jnp.where(pos0 + tm < seq, un_ref[...], 0.0)
    tu = tm // ROW_UNITS
    for r0 in range(0, tm, tu):
        t = pos0 + r0 + lax.broadcasted_iota(i32, (tu, 1), 0)
        mixer = [attn_ref[r0:r0 + tu, :]]
        for g, w in enumerate(POOL_WINDOWS):
            hw = w // 2
            cols = slice(g * POOL_GROUP, (g + 1) * POOL_GROUP)
            base = halo + r0
            acc = ubuf[base - hw:base - hw + tu, cols]
            for o in range(-hw + 1, hw):
                acc = acc + ubuf[base + o:base + o + tu, cols]
            cnt = (jnp.minimum(t + hw, seq) - jnp.maximum(t - hw, 0)).astype(f32)
            mixed = acc / cnt - ubuf[base:base + tu, cols]
            y = jnp.dot(mixed.astype(bf16), wp_ref[g], preferred_element_type=f32) * sc_ref[:, cols]
            mixer.append(y.astype(bf16))
        proj = jnp.dot(jnp.concatenate(mixer, axis=1), wo_ref[...], preferred_element_type=f32)
        _post_mixer(_load_rows(x_ref, r0, tu) + proj, r0, g_ref, wr_ref, xo_ref, h_ref, aff_ref)


def _even_out(attn2, u2, xt, wo_bf, wp_bf, scale, g, wr, seq, b):
    n = _num_tokens(xt)
    d = SUBLANES * LANES
    tm = ROW_TILE
    halo = SUBLANES
    row = lambda i: (i, 0)
    full2 = lambda i: (0, 0)
    nhb = n // halo
    out_specs, out_shape = _post_specs(n, seq, tm, b)
    return pl.pallas_call(
        functools.partial(_even_out_kernel, seq=seq),
        grid=(n // tm,),
        in_specs=[pl.BlockSpec((tm, A_Q), row), pl.BlockSpec((tm, POOL_WIDTH), row),
                  pl.BlockSpec((halo, POOL_WIDTH), lambda i: (jnp.maximum(i * (tm // halo) - 1, 0), 0)),
                  pl.BlockSpec((halo, POOL_WIDTH), lambda i: (jnp.minimum((i + 1) * (tm // halo), nhb - 1), 0)),
                  _rows_spec(xt, tm), pl.BlockSpec(wo_bf.shape, full2),
                  pl.BlockSpec(wp_bf.shape, lambda i: (0, 0, 0)), pl.BlockSpec((1, POOL_WIDTH), full2),
                  pl.BlockSpec((1, d), full2), pl.BlockSpec(wr.shape, full2)],
        out_specs=out_specs, out_shape=out_shape,
        scratch_shapes=[pltpu.VMEM((tm + 2 * halo, POOL_WIDTH), f32)],
        compiler_params=pltpu.CompilerParams(dimension_semantics=("arbitrary",)),
        name="even_out_proj",
    )(attn2, u2, u2, u2, xt, wo_bf, wp_bf, scale, g, wr)


def _odd_in_kernel(x_ref, g_ref, w_ref, qg_ref, kg_ref, cos_ref, sin_ref, q_ref, k_ref, v_ref):
    scale = HEAD_DIM_C ** -0.5 * LOG2_E
    tm = q_ref.shape[0]
    tu = tm // ROW_UNITS
    for r0 in range(0, tm, tu):
        rows = slice(r0, r0 + tu)
        h = _rms(_load_rows(x_ref, r0, tu), g_ref[...]).astype(bf16)
        proj = jnp.dot(h, w_ref[...], preferred_element_type=f32)
        cos = cos_ref[rows, :]
        sin = sin_ref[rows, :]
        for c in range(N_HEADS_C):
            qc = _rms(proj[:, c * LANES:(c + 1) * LANES], qg_ref[...])
            q_ref[rows, c * LANES:(c + 1) * LANES] = (_rope(qc, cos, sin) * scale).astype(bf16)
        for c in range(N_KV_C):
            kc = _rms(proj[:, C_Q + c * LANES:C_Q + (c + 1) * LANES], kg_ref[...])
            k_ref[rows, c * LANES:(c + 1) * LANES] = _rope(kc, cos, sin).astype(bf16)
            vc = proj[:, C_Q + C_KV + c * LANES:C_Q + C_KV + (c + 1) * LANES]
            v_ref[rows, 2 * c * LANES:(2 * c + 1) * LANES] = vc.astype(bf16)
            v_ref[rows, (2 * c + 1) * LANES:(2 * c + 2) * LANES] = jnp.ones((tu, LANES), bf16)


def _odd_in(xt, g, w_bf, qg, kg, cos, sin, seq):
    n = _num_tokens(xt)
    d = SUBLANES * LANES
    tm = ROW_TILE
    per_seq = seq // tm
    row = lambda i: (i, 0)
    tab = lambda i: (i % per_seq, 0)
    full = lambda i: (0, 0)
    return pl.pallas_call(
        _odd_in_kernel,
        grid=(n // tm,),
        in_specs=[_tiled_spec(tm), pl.BlockSpec((1, d), full), pl.BlockSpec(w_bf.shape, full),
                  pl.BlockSpec((1, HEAD_DIM_C), full), pl.BlockSpec((1, HEAD_DIM_C), full),
                  pl.BlockSpec((tm, LANES), tab), pl.BlockSpec((tm, LANES), tab)],
        out_specs=[pl.BlockSpec((tm, C_Q), row), pl.BlockSpec((tm, C_KV), row), pl.BlockSpec((tm, 2 * C_KV), row)],
        out_shape=[jax.ShapeDtypeStruct((n, C_Q), bf16), jax.ShapeDtypeStruct((n, C_KV), bf16),
                   jax.ShapeDtypeStruct((n, 2 * C_KV), bf16)],
        compiler_params=pltpu.CompilerParams(dimension_semantics=("arbitrary",)),
        name="odd_in_proj",
    )(xt, g, w_bf, qg, kg, cos, sin)


def _flash_kernel(q_ref, k_ref, v_ref, o_ref, m_sc, acc_sc, *, tk):
    tq = q_ref.shape[0]
    rows_per_unit = tq // FLASH_ROW_SPLIT
    m_sc[...] = jnp.full(m_sc.shape, NEG_INF, f32)
    acc_sc[...] = jnp.zeros(acc_sc.shape, f32)

    def key_tile(ki, carry):
        k0 = pl.multiple_of(ki * tk, tk)
        k = k_ref[pl.ds(k0, tk), :]
        v = v_ref[pl.ds(k0, tk), :]
        for g in range(GROUP_C):
            for r0 in range(0, tq, rows_per_unit):
                rows = slice(r0, r0 + rows_per_unit)
                qg = q_ref[rows, g * LANES:(g + 1) * LANES]
                s = lax.dot_general(qg, k, (((1,), (1,)), ((), ())), preferred_element_type=f32)
                m_prev = m_sc[g, rows]
                m_new = jnp.maximum(m_prev, jnp.max(s, axis=1, keepdims=True))
                p = jnp.exp2(s - m_new).astype(bf16)
                acc_sc[g, rows] = (jnp.exp2(m_prev - m_new) * acc_sc[g, rows]
                                   + jnp.dot(p, v, preferred_element_type=f32))
                m_sc[g, rows] = m_new
        return carry

    lax.fori_loop(0, k_ref.shape[0] // tk, key_tile, 0)
    for g in range(GROUP_C):
        acc = acc_sc[g]
        o_ref[:, g * LANES:(g + 1) * LANES] = (acc[:, :HEAD_DIM_C] / acc[:, HEAD_DIM_C:]).astype(bf16)


def _flash(q3, k3, v3):
    b, seq, _ = q3.shape
    tq, tk = FLASH_TQ, min(FLASH_TK, seq)
    gw = GROUP_C * HEAD_DIM_C
    return pl.pallas_call(
        functools.partial(_flash_kernel, tk=tk),
        grid=(b, N_KV_C, seq // tq),
        in_specs=[pl.BlockSpec((None, tq, gw), lambda bi, j, qi: (bi, qi, j)),
                  pl.BlockSpec((None, seq, HEAD_DIM_C), lambda bi, j, qi: (bi, 0, j)),
                  pl.BlockSpec((None, seq, 2 * HEAD_DIM_C), lambda bi, j, qi: (bi, 0, j))],
        out_specs=pl.BlockSpec((None, tq, gw), lambda bi, j, qi: (bi, qi, j)),
        out_shape=jax.ShapeDtypeStruct((b, seq, C_Q), bf16),
        scratch_shapes=[pltpu.VMEM((GROUP_C, tq, 1), f32), pltpu.VMEM((GROUP_C, tq, 2 * HEAD_DIM_C), f32)],
        compiler_params=pltpu.CompilerParams(dimension_semantics=("arbitrary", "arbitrary", "arbitrary")),
        name="dense_attention",
    )(q3, k3, v3)


def _odd_out_kernel(attn_ref, x_ref, wo_ref, g_ref, wr_ref, xo_ref, h_ref, aff_ref):
    tm = attn_ref.shape[0]
    tu = tm // ROW_UNITS
    for r0 in range(0, tm, tu):
        proj = jnp.dot(attn_ref[r0:r0 + tu, :], wo_ref[...], preferred_element_type=f32)
        _post_mixer(_load_rows(x_ref, r0, tu) + proj, r0, g_ref, wr_ref, xo_ref, h_ref, aff_ref)


def _odd_out(attn2, xt, wo_bf, g, wr, seq, b):
    n = _num_tokens(xt)
    d = SUBLANES * LANES
    tm = ROW_TILE
    row = lambda i: (i, 0)
    full2 = lambda i: (0, 0)
    out_specs, out_shape = _post_specs(n, seq, tm, b)
    return pl.pallas_call(
        _odd_out_kernel,
        grid=(n // tm,),
        in_specs=[pl.BlockSpec((tm, C_Q), row), _tiled_spec(tm), pl.BlockSpec(wo_bf.shape, full2),
                  pl.BlockSpec((1, d), full2), pl.BlockSpec(wr.shape, full2)],
        out_specs=out_specs, out_shape=out_shape,
        compiler_params=pltpu.CompilerParams(dimension_semantics=("arbitrary",)),
        name="odd_out_proj",
    )(attn2, xt, wo_bf, g, wr)


def _topk_kernel(aff_ref, idx_ref, *, cap):
    ne, r, _ = aff_ref.shape
    er = ne * r
    shift = r.bit_length() - 1
    aff = aff_ref[...]

    def count(mask):
        return jnp.sum(jnp.sum(mask.astype(f32), axis=2, keepdims=True), axis=1, keepdims=True)

    def bit_step(it, bits):
        cand = bits | jnp.left_shift(jnp.int32(1), 30 - it)
        return jnp.where(count(aff >= lax.bitcast_convert_type(cand, f32)) >= cap, cand, bits)

    bits = lax.fori_loop(0, 31, bit_step, jnp.zeros((ne, 1, 1), i32))
    gt = aff >= lax.bitcast_convert_type(bits + 1, f32)
    eq = (aff >= lax.bitcast_convert_type(bits, f32)) & jnp.logical_not(gt)
    need = cap - count(gt)

    li = lax.broadcasted_iota(i32, (LANES, LANES), 0)
    lj = lax.broadcasted_iota(i32, (LANES, LANES), 1)
    ri = lax.broadcasted_iota(i32, (er, er), 0)
    rj = lax.broadcasted_iota(i32, (er, er), 1)
    rows_before = (((ri >> shift) == (rj >> shift)) & (rj < ri)).astype(bf16)
    ones_sq = jnp.ones((LANES, LANES), bf16)

    def row_stats(mask3, inclusive):
        x = mask3.astype(bf16).reshape(er, LANES)
        tri = ((li <= lj) if inclusive else (li < lj)).astype(bf16)
        within = jnp.dot(x, tri, preferred_element_type=f32)
        tot = jnp.dot(x, ones_sq, preferred_element_type=f32)
        before = jnp.dot(rows_before, tot.astype(bf16), preferred_element_type=f32)
        return within, tot, before

    w_eq, _, b_eq = row_stats(eq, False)
    sel = gt | (eq & ((w_eq + b_eq).reshape(ne, r, LANES) < need))
    within, tot, before = row_stats(sel, True)

    lane_f = lax.broadcasted_iota(i32, (r, LANES), 1).astype(f32)
    pad = LANES - r
    for e in range(ne):
        rows = slice(e * r, (e + 1) * r)
        pre, tt, we = before[rows], tot[rows], within[rows]
        inc = pre + tt
        if pad:
            we = jnp.concatenate([we, jnp.zeros((pad, LANES), f32)], axis=0)
        we_t = we.T.astype(bf16)
        row_of, rank, hot = [], [], []
        for c in range(cap // LANES):
            s = lane_f + float(c * LANES)
            done = inc <= s
            row_of.append(jnp.sum(done.astype(f32), axis=0, keepdims=True))
            rank.append(s[0:1, :] - jnp.sum(jnp.where(done, tt, 0.0), axis=0, keepdims=True))
            hot.append(((pre <= s) & jnp.logical_not(done)).astype(bf16))
        hot = jnp.concatenate(hot, axis=1)
        if pad:
            hot = jnp.concatenate([hot, jnp.zeros((pad, cap), bf16)], axis=0)
        run = jnp.dot(we_t, hot, preferred_element_type=f32)
        lane_of = jnp.sum((run <= jnp.concatenate(rank, axis=1)).astype(f32), axis=0, keepdims=True)
        idx_ref[e] = (jnp.concatenate(row_of, axis=1) * LANES + lane_of).astype(i32)


def _topk(aff4, cap):
    b, ne, r, _ = aff4.shape
    assert r & (r - 1) == 0 and r <= LANES and r % SUBLANES == 0 and cap % LANES == 0
    return pl.pallas_call(
        functools.partial(_topk_kernel, cap=cap),
        grid=(b,),
        in_specs=[pl.BlockSpec((None, ne, r, LANES), lambda bi: (bi, 0, 0, 0))],
        out_specs=pl.BlockSpec((None, ne, 1, cap), lambda bi: (bi, 0, 0, 0)),
        out_shape=jax.ShapeDtypeStruct((b, ne, 1, cap), i32),
        compiler_params=pltpu.CompilerParams(dimension_semantics=("arbitrary",)),
        name="expert_choice_topk",
    )(aff4)


def _ffn_kernel(idx_ref, idxn_ref, idxp_ref, affp_ref, h_ref, x_ref, wg_ref, wu_ref, wd_ref, out_ref,
                acc, xsbuf, xs, yacc, ydone, gsem, xsem, *, seq, cap, ne):
    n, j = pl.program_id(0), pl.program_id(1)
    n_exp, nj = pl.num_programs(0) - 1, pl.num_programs(1)
    tile = SUBLANES
    rows_per_step = cap // nj
    batch_cur = jnp.minimum(n, n_exp - 1) // ne
    batch_next = jnp.minimum(n + 1, n_exp - 1) // ne
    batch_prev = jnp.maximum(n - 1, 0) // ne

    def token(ref, t):
        return ref.at[pl.ds(pl.multiple_of(t * tile, tile), tile), :]

    def gather_start(token_row, s):
        pltpu.make_async_copy(token(h_ref, token_row), token(xsbuf, s), gsem.at[0]).start()

    def gather_wait():
        pltpu.make_async_copy(h_ref.at[pl.ds(0, cap * tile), :], xsbuf, gsem.at[0]).wait()

    def acc_load(batch):
        cp = pltpu.make_async_copy(x_ref.at[pl.ds(batch * seq * tile, seq * tile), :], acc, xsem.at[0])
        cp.start()
        cp.wait()

    def scatter_prev_rows():
        for g0 in range(0, rows_per_step, SCATTER_GROUP):
            ss = [j * rows_per_step + g0 + k for k in range(SCATTER_GROUP)]
            ts = [idxp_ref[0, s] for s in ss]
            new = [token(acc, t)[...] + affp_ref[0, t] * token(ydone, s)[...] for s, t in zip(ss, ts)]
            for t, v in zip(ts, new):
                token(acc, t)[...] = v

    @pl.when((n == 0) & (j == 0))
    def _():
        def trip(i, c):
            for k in range(ROW_UNROLL):
                s = i * ROW_UNROLL + k
                gather_start(idx_ref[0, s], s)
            return c
        lax.fori_loop(0, cap // ROW_UNROLL, trip, 0)
        ydone[...] = jnp.zeros(ydone.shape, f32)
        acc_load(0)

    @pl.when((j == 0) & (n < n_exp))
    def _():
        gather_wait()
        xs[...] = _load_rows(xsbuf).astype(bf16)
        yacc[...] = jnp.zeros(yacc.shape, f32)

    @pl.when(n < n_exp)
    def _():
        for k in range(rows_per_step):
            s = j * rows_per_step + k
            gather_start(batch_next * seq + idxn_ref[0, s], s)
        xv = xs[...]
        gate = jnp.dot(xv, wg_ref[...].astype(bf16), preferred_element_type=f32)
        up = jnp.dot(xv, wu_ref[...].astype(bf16), preferred_element_type=f32)
        hm = (gate * jax.nn.sigmoid(gate) * up).astype(bf16)
        yacc[...] += jnp.dot(hm, wd_ref[...].astype(bf16), preferred_element_type=f32)
        scatter_prev_rows()

    @pl.when(n == n_exp)
    def _():
        scatter_prev_rows()

    @pl.when((j == nj - 1) & (n < n_exp))
    def _():
        _store_rows(ydone, yacc[...])

    @pl.when((j == nj - 1) & (n == n_exp - 1))
    def _():
        gather_wait()

    @pl.when((j == nj - 1) & (n > 0) & (n % ne == 0))
    def _():
        cp = pltpu.make_async_copy(acc, out_ref.at[pl.ds(batch_prev * seq * tile, seq * tile), :], xsem.at[1])
        cp.start()
        cp.wait()

        @pl.when(n < n_exp)
        def _():
            acc_load(batch_cur)


def _ffn(idx3, aff3, ht, xt, wg, wu, wd, layer, b, seq, cap):
    n = _num_tokens(xt)
    d = SUBLANES * LANES
    ne = N_EXPERTS
    n_exp = b * ne
    tf = FF_TILE
    nj = wg.shape[-1] // tf
    cur = lambda i, j: (jnp.minimum(i, n_exp - 1), 0, 0)
    nxt = lambda i, j: (jnp.minimum(i + 1, n_exp - 1), 0, 0)
    prv = lambda i, j: (jnp.maximum(i - 1, 0), 0, 0)
    expert = lambda i: jnp.minimum(i, n_exp - 1) % ne
    col = lambda i, j: jnp.where(i < n_exp, j, nj - 1)
    smem = pltpu.SMEM
    return pl.pallas_call(
        functools.partial(_ffn_kernel, seq=seq, cap=cap, ne=ne),
        grid=(n_exp + 1, nj),
        in_specs=[pl.BlockSpec((None, 1, cap), cur, memory_space=smem),
                  pl.BlockSpec((None, 1, cap), nxt, memory_space=smem),
                  pl.BlockSpec((None, 1, cap), prv, memory_space=smem),
                  pl.BlockSpec((None, 1, seq), prv, memory_space=smem),
                  pl.BlockSpec(memory_space=pl.ANY), pl.BlockSpec(memory_space=pl.ANY),
                  pl.BlockSpec((None, None, d, tf), lambda i, j: (layer, expert(i), 0, col(i, j))),
                  pl.BlockSpec((None, None, d, tf), lambda i, j: (layer, expert(i), 0, col(i, j))),
                  pl.BlockSpec((None, None, tf, d), lambda i, j: (layer, expert(i), col(i, j), 0))],
        out_specs=pl.BlockSpec(memory_space=pl.ANY),
        out_shape=jax.ShapeDtypeStruct((n * SUBLANES, LANES), f32),
        scratch_shapes=[pltpu.VMEM((seq * SUBLANES, LANES), f32), pltpu.VMEM((cap * SUBLANES, LANES), f32),
                        pltpu.VMEM((cap, d), bf16), pltpu.VMEM((cap, d), f32),
                        pltpu.VMEM((cap * SUBLANES, LANES), f32),
                        pltpu.SemaphoreType.DMA((1,)), pltpu.SemaphoreType.DMA((2,))],
        compiler_params=pltpu.CompilerParams(
            dimension_semantics=("arbitrary", "arbitrary"),
            vmem_limit_bytes=VMEM_LIMIT_FFN),
        name="expert_ffn",
    )(idx3, idx3, idx3, aff3, ht, xt, wg, wu, wd)


def _final_norm_kernel(x_ref, g_ref, o_ref):
    o_ref[...] = _rms(_load_rows(x_ref), g_ref[...])


def _final_norm(xt, g):
    n = _num_tokens(xt)
    d = SUBLANES * LANES
    tm = ROW_TILE
    return pl.pallas_call(
        _final_norm_kernel,
        grid=(n // tm,),
        in_specs=[_tiled_spec(tm), pl.BlockSpec((1, d), lambda i: (0, 0))],
        out_specs=pl.BlockSpec((tm, d), lambda i: (i, 0)),
        out_shape=jax.ShapeDtypeStruct((n, d), f32),
        compiler_params=pltpu.CompilerParams(dimension_semantics=("arbitrary",)),
        name="final_norm",
    )(xt, g)


def kernel(x, norm_mix, norm_ffn, norm_final, a_w_in, a_w_out, a_sink, b_w_pool, b_scale,
           c_w_qkv, c_q_norm, c_k_norm, c_w_out, moe_router, moe_w_gate, moe_w_up, moe_w_down):
    b, seq, d = x.shape
    depth = norm_mix.shape[0]
    n = b * seq
    cap = CAPACITY_FACTOR * seq // N_EXPERTS
    assert d == SUBLANES * LANES
    assert seq % ROW_TILE == 0 and seq % FLASH_TQ == 0 and seq % min(FLASH_TK, seq) == 0 and seq % WIN_TQ == 0
    rows_per_step = cap // (moe_w_gate.shape[-1] // FF_TILE)
    assert rows_per_step % SCATTER_GROUP == 0 and cap % ROW_UNROLL == 0

    t = np.arange(seq)
    cos_a, sin_a = _rope_tables(t, t, HEAD_DIM_A)
    cos_c, sin_c = _rope_tables(t // GRID_W, t % GRID_W, AXIAL_DIM)
    wr_pad = jnp.pad(moe_router, ((0, 0), (0, 0), (0, LANES - N_EXPERTS)))

    xt = x.reshape(n, d)
    for i in range(depth):
        j = i // 2
        g_mix = norm_mix[i].reshape(1, d)
        g_ffn = norm_ffn[i].reshape(1, d)
        if i % 2 == 0:
            q, ka, kb, va, vb, u = _even_in(xt, g_mix, a_w_in[j].astype(bf16), cos_a, sin_a, seq)
            r3 = lambda a: a.reshape(b, seq, a.shape[-1])
            attn = _win_attn(a_sink[j], r3(q), r3(ka), r3(kb), r3(va), r3(vb))
            xt, ht, aff = _even_out(attn.reshape(n, A_Q), u, xt, a_w_out[j].astype(bf16),
                                    b_w_pool[j].astype(bf16), b_scale[j].reshape(1, POOL_WIDTH),
                                    g_ffn, wr_pad[i], seq, b)
        else:
            q, k, v = _odd_in(xt, g_mix, c_w_qkv[j].astype(bf16), c_q_norm[j].reshape(1, HEAD_DIM_C),
                              c_k_norm[j].reshape(1, HEAD_DIM_C), cos_c, sin_c, seq)
            attn = _flash(q.reshape(b, seq, C_Q), k.reshape(b, seq, C_KV), v.reshape(b, seq, 2 * C_KV))
            xt, ht, aff = _odd_out(attn.reshape(n, C_Q), xt, c_w_out[j].astype(bf16), g_ffn, wr_pad[i], seq, b)
        idx = _topk(aff.reshape(b, N_EXPERTS, seq // LANES, LANES), cap)
        xt = _ffn(idx.reshape(b * N_EXPERTS, 1, cap), aff.reshape(b * N_EXPERTS, 1, seq), ht, xt,
                  moe_w_gate, moe_w_up, moe_w_down, i, b, seq, cap)
    return _final_norm(xt, norm_final.reshape(1, d)).reshape(b, seq, d)
```
